```python
import jax, jax.numpy as jnp
from jax import lax
import numpy as np

D_MODEL = 1024
BATCH = 8
SEQ = 2048
DEPTH = 2

N_MIXERS = 2
N_LAYERS_A = (DEPTH + 1) // 2
N_LAYERS_B = DEPTH // 2
D_CONV = D_MODEL
CONV_WIDTH = 31
HEAD_DIM = 64
N_HEADS_B = D_MODEL // HEAD_DIM
D_ATTN = N_HEADS_B * HEAD_DIM
DILATED_GROUPS = ((128, 1), (512, 4), (2048, 16))
N_GROUPS = len(DILATED_GROUPS)
BLOCK = 128
IN_COLS_B = N_GROUPS * 3 * D_ATTN + D_ATTN
NORM_EPS = 1e-6
NEG_INF = -1e30

kernel_name = "hybrid_conv_dilated_attn_adaln"


def rms_norm(x, g):
    xf = x.astype(jnp.float32)
    y = xf * lax.rsqrt(jnp.mean(xf * xf, axis=-1, keepdims=True) + NORM_EPS)
    return (y * g.astype(jnp.float32)).astype(x.dtype)


def layer_norm(x, g, b):
    xf = x.astype(jnp.float32)
    mu = jnp.mean(xf, axis=-1, keepdims=True)
    xc = xf - mu
    y = xc * lax.rsqrt(jnp.mean(xc * xc, axis=-1, keepdims=True) + NORM_EPS)
    return (y * g.astype(jnp.float32) + b.astype(jnp.float32)).astype(x.dtype)


def alibi_slopes(n_heads):
    return jnp.exp2(-8.0 * jnp.arange(1, n_heads + 1, dtype=jnp.float32) / n_heads)


def ada_modulation(c, w, b):
    mod = jax.nn.silu(c) @ w + b
    shift, scale, gate = jnp.split(mod, 3, axis=-1)
    return shift[:, None, :], scale[:, None, :], gate[:, None, :]


def conformer_conv_mixer(h, w_in, conv_w, conv_b, ln_g, ln_b, w_out):
    proj = h @ w_in
    val, glu_gate, z = jnp.split(proj, 3, axis=-1)
    u = val * jax.nn.sigmoid(glu_gate)
    u = lax.conv_general_dilated(
        u, conv_w[:, None, :], window_strides=(1,), padding=[(CONV_WIDTH - 1, 0)],
        dimension_numbers=("NWC", "WIO", "NWC"), feature_group_count=D_CONV) + conv_b
    u = jax.nn.silu(layer_norm(u, ln_g, ln_b))
    return (u * jax.nn.silu(z)) @ w_out


def dilated_window_group(q, k, v, window, dilation, slopes):
    B, S, H, Dh = q.shape
    n_steps = window // dilation
    L = S // dilation
    nb = -(-L // BLOCK)
    Lp = nb * BLOCK
    N = B * dilation

    def to_classes(t):
        t = t.reshape(B, L, dilation, H, Dh).transpose(0, 2, 1, 3, 4).reshape(N, L, H, Dh)
        return jnp.pad(t, ((0, 0), (0, Lp - L), (0, 0), (0, 0)))

    def band(t):
        t = jnp.pad(t, ((0, 0), (BLOCK, 0), (0, 0), (0, 0))).reshape(N, nb + 1, BLOCK, H, Dh)
        return jnp.concatenate([t[:, :-1], t[:, 1:]], axis=2)

    qb = to_classes(q).reshape(N, nb, BLOCK, H, Dh)
    kb = band(to_classes(k))
    vb = band(to_classes(v))

    s = jnp.einsum("nbqhd,nbkhd->nhbqk", qb, kb) * (Dh ** -0.5)
    qi = jnp.arange(BLOCK)[:, None]
    kj = jnp.arange(2 * BLOCK)[None, :]
    steps = qi + BLOCK - kj
    key_idx = jnp.arange(nb)[:, None, None] * BLOCK + kj[None] - BLOCK
    valid = (steps >= 0) & (steps <= n_steps) & (key_idx >= 0)
    dist = (steps * dilation).astype(jnp.float32)
    s = s - slopes[:, None, None, None] * dist
    s = jnp.where(valid, s, NEG_INF)
    lse = jax.nn.logsumexp(s, axis=-1)
    p = jnp.exp(s - lse[..., None])
    o = jnp.einsum("nhbqk,nbkhd->nbqhd", p, vb)

    def from_classes(t):
        t = t.reshape((B, dilation, Lp) + t.shape[3:])[:, :, :L]
        return jnp.moveaxis(t, 1, 2).reshape((B, S) + t.shape[3:])

    return from_classes(o), from_classes(jnp.moveaxis(lse, 1, -1))


def dilated_attention_mixer(h, w_in, q_norm, k_norm, w_out):
    B, S, _ = h.shape
    proj = h @ w_in
    qkv = proj[..., :N_GROUPS * 3 * D_ATTN].reshape(B, S, N_GROUPS, 3, N_HEADS_B, HEAD_DIM)
    z = proj[..., N_GROUPS * 3 * D_ATTN:]
    slopes = alibi_slopes(N_HEADS_B)
    outs, lses = [], []
    for g, (window, dilation) in enumerate(DILATED_GROUPS):
        q = rms_norm(qkv[:, :, g, 0], q_norm[g]).astype(jnp.float32)
        k = rms_norm(qkv[:, :, g, 1], k_norm[g]).astype(jnp.float32)
        v = qkv[:, :, g, 2].astype(jnp.float32)
        o, lse = dilated_window_group(q, k, v, window, dilation, slopes)
        outs.append(o)
        lses.append(lse)
    wts = jax.nn.softmax(jnp.stack(lses), axis=0)
    o = jnp.sum(wts[..., None] * jnp.stack(outs), axis=0)
    o = o.reshape(B, S, D_ATTN).astype(h.dtype)
    return (o * jax.nn.silu(z)) @ w_out


def setup_inputs(seed: int = 0) -> dict:
    key = jax.random.key(seed)
    ks = jax.random.split(key, 16)
    f32 = jnp.float32
    nrm = lambda k, shape, s: jax.random.normal(k, shape, f32) * s
    return {
        "x": nrm(ks[0], (BATCH, SEQ, D_MODEL), 1.0),
        "c": nrm(ks[1], (BATCH, D_MODEL), 1.0),
        "norm_g": 1.0 + nrm(ks[2], (DEPTH, D_MODEL), 0.05),
        "ada_w": nrm(ks[3], (DEPTH, D_MODEL, 3 * D_MODEL), D_MODEL ** -0.5),
        "ada_b": nrm(ks[4], (DEPTH, 3 * D_MODEL), 0.02),
        "a_w_in": nrm(ks[5], (N_LAYERS_A, D_MODEL, 3 * D_CONV), D_MODEL ** -0.5),
        "a_conv_w": nrm(ks[6], (N_LAYERS_A, CONV_WIDTH, D_CONV), CONV_WIDTH ** -0.5),
        "a_conv_b": nrm(ks[7], (N_LAYERS_A, D_CONV), 0.02),
        "a_ln_g": 1.0 + nrm(ks[8], (N_LAYERS_A, D_CONV), 0.05),
        "a_ln_b": nrm(ks[9], (N_LAYERS_A, D_CONV), 0.02),
        "a_w_out": nrm(ks[10], (N_LAYERS_A, D_CONV, D_MODEL), D_CONV ** -0.5),
        "b_w_in": nrm(ks[11], (N_LAYERS_B, D_MODEL, IN_COLS_B), D_MODEL ** -0.5),
        "b_q_norm": 1.0 + nrm(ks[12], (N_LAYERS_B, N_GROUPS, HEAD_DIM), 0.05),
        "b_k_norm": 1.0 + nrm(ks[13], (N_LAYERS_B, N_GROUPS, HEAD_DIM), 0.05),
        "b_w_out": nrm(ks[14], (N_LAYERS_B, D_ATTN, D_MODEL), D_ATTN ** -0.5),
    }


def reference(x, c, norm_g, ada_w, ada_b, a_w_in, a_conv_w, a_conv_b, a_ln_g, a_ln_b, a_w_out,
              b_w_in, b_q_norm, b_k_norm, b_w_out):
    for layer in range(DEPTH):
        shift, scale, gate = ada_modulation(c, ada_w[layer], ada_b[layer])
        h = rms_norm(x, norm_g[layer]) * (1.0 + scale) + shift
        j = layer // N_MIXERS
        if layer % N_MIXERS == 0:
            y = conformer_conv_mixer(h, a_w_in[j], a_conv_w[j], a_conv_b[j], a_ln_g[j], a_ln_b[j], a_w_out[j])
        else:
            y = dilated_attention_mixer(h, b_w_in[j], b_q_norm[j], b_k_norm[j], b_w_out[j])
        x = x + gate * y
    return x
```

```python
import functools

import jax
import jax.numpy as jnp
import numpy as np
from jax import lax
from jax.experimental import pallas as pl
from jax.experimental.pallas import tpu as pltpu

F32 = jnp.float32
BF16 = jnp.bfloat16

D_MODEL = 1024
CONV_WIDTH = 31
HEAD_DIM = 64
N_HEADS = 16
N_PAIRS = N_HEADS // 2
DILATED_GROUPS = ((128, 1), (512, 4), (2048, 16))
BLOCK = 128
NORM_EPS = 1e-6
NEG_INF = -1e30
LANES = 128
VMEM_LIMIT = 56 * 1024 * 1024

TOK_TILE = 512
Q_BLOCKS = TOK_TILE // BLOCK
L0_TILE = 256
HALO = 32
CONV_ROWS = 128


def _sigmoid(x):
    return 1.0 / (1.0 + jnp.exp(-x))


def _silu(x):
    return x * _sigmoid(x)


def _cparams(sem):
    return pltpu.CompilerParams(dimension_semantics=sem, vmem_limit_bytes=VMEM_LIMIT)


def _ada_kernel(c_ref, w_ref, b_ref, o_ref):
    c = c_ref[...]
    sc = _silu(c).astype(BF16)
    o_ref[...] = jnp.dot(sc, w_ref[...].astype(BF16), preferred_element_type=F32) + b_ref[...]


def _ada_modulation(c, ada_w, ada_b):
    depth, d, d3 = ada_w.shape
    nb = c.shape[0]
    cols = 768
    out = pl.pallas_call(
        _ada_kernel,
        grid=(depth, d3 // cols),
        in_specs=[
            pl.BlockSpec((nb, d), lambda l, j: (0, 0)),
            pl.BlockSpec((None, d, cols), lambda l, j: (l, 0, j)),
            pl.BlockSpec((None, 1, cols), lambda l, j: (l, 0, j)),
        ],
        out_specs=pl.BlockSpec((None, nb, cols), lambda l, j: (l, 0, j)),
        out_shape=jax.ShapeDtypeStruct((depth, nb, d3), F32),
        compiler_params=_cparams(("arbitrary", "arbitrary")),
        name="ada_modulation",
    )(c, ada_w, ada_b.reshape(depth, 1, d3))
    return out.reshape(depth, nb, 3, d)


def _layer0_kernel(x_ref, mod0_ref, mod1_ref, g0_ref, g1_ref, win_ref, cw_ref, cb_ref, lg_ref, lb_ref,
                   wout_ref, x1_ref, h1_ref, ubuf, cbuf):
    ts = x_ref.shape[0]
    d = x_ref.shape[1]

    @pl.when(pl.program_id(1) == 0)
    def _():
        ubuf[0:HALO, :] = jnp.zeros((HALO, d), F32)

    x = x_ref[...]
    shift, scale, gate = mod0_ref[0:1, :], mod0_ref[1:2, :], mod0_ref[2:3, :]
    ms = jnp.mean(x * x, axis=-1, keepdims=True)
    h = (x * lax.rsqrt(ms + NORM_EPS) * g0_ref[...]) * (1.0 + scale) + shift
    proj = jnp.dot(h.astype(BF16), win_ref[...], preferred_element_type=F32)
    u = proj[:, 0:d] * _sigmoid(proj[:, d:2 * d])
    sz = _silu(proj[:, 2 * d:3 * d])
    ubuf[HALO:HALO + ts, :] = u

    off = HALO - (CONV_WIDTH - 1)
    for ls in range(d // LANES):
        lsl = slice(ls * LANES, (ls + 1) * LANES)
        for rc in range(ts // CONV_ROWS):
            r0 = rc * CONV_ROWS
            acc = jnp.broadcast_to(cb_ref[0:1, lsl], (CONV_ROWS, LANES))
            for j in range(CONV_WIDTH):
                acc = acc + cw_ref[j:j + 1, lsl] * ubuf[r0 + off + j:r0 + off + j + CONV_ROWS, lsl]
            cbuf[r0:r0 + CONV_ROWS, lsl] = acc
    ubuf[0:HALO, :] = ubuf[ts:ts + HALO, :]

    cv = cbuf[...]
    mu = jnp.mean(cv, axis=-1, keepdims=True)
    xc = cv - mu
    var = jnp.mean(xc * xc, axis=-1, keepdims=True)
    y = xc * lax.rsqrt(var + NORM_EPS) * lg_ref[...] + lb_ref[...]
    a = (_silu(y) * sz).astype(BF16)
    out = jnp.dot(a, wout_ref[...], preferred_element_type=F32)
    x1 = x + gate * out
    x1_ref[...] = x1

    shift1, scale1 = mod1_ref[0:1, :], mod1_ref[1:2, :]
    ms1 = jnp.mean(x1 * x1, axis=-1, keepdims=True)
    h1 = (x1 * lax.rsqrt(ms1 + NORM_EPS) * g1_ref[...]) * (1.0 + scale1) + shift1
    h1_ref[...] = h1.astype(BF16)


def _layer0(x, mod, norm_g, w_in, conv_w, conv_b, ln_g, ln_b, w_out):
    nb, s, d = x.shape
    ts = L0_TILE
    row = lambda a: a.reshape(1, d)
    const = lambda shape: pl.BlockSpec(shape, lambda b, t: (0,) * len(shape))
    return pl.pallas_call(
        _layer0_kernel,
        grid=(nb, s // ts),
        in_specs=[
            pl.BlockSpec((None, ts, d), lambda b, t: (b, t, 0)),
            pl.BlockSpec((None, None, 3, d), lambda b, t: (0, b, 0, 0)),
            pl.BlockSpec((None, None, 3, d), lambda b, t: (1, b, 0, 0)),
            const((1, d)), const((1, d)),
            const((d, 3 * d)),
            const((CONV_WIDTH, d)), const((1, d)), const((1, d)), const((1, d)),
            const((d, d)),
        ],
        out_specs=[
            pl.BlockSpec((None, ts, d), lambda b, t: (b, t, 0)),
            pl.BlockSpec((None, ts, d), lambda b, t: (b, t, 0)),
        ],
        out_shape=[jax.ShapeDtypeStruct((nb, s, d), F32), jax.ShapeDtypeStruct((nb, s, d), BF16)],
        scratch_shapes=[pltpu.VMEM((HALO + ts, d), F32), pltpu.VMEM((ts, d), F32)],
        compiler_params=_cparams(("arbitrary", "arbitrary")),
        name="layer0_conv_mixer",
    )(x, mod, mod, row(norm_g[0]), row(norm_g[1]), w_in.astype(BF16), conv_w, row(conv_b), row(ln_g),
      row(ln_b), w_out.astype(BF16))


def _proj_kernel(*refs, classes_per_step, with_z):
    if with_z:
        h_ref, wq_ref, wkt_ref, wv_ref, gq_ref, gk_ref, wz_ref, q_ref, kt_ref, v_ref, z_ref = refs
    else:
        h_ref, wq_ref, wkt_ref, wv_ref, gq_ref, gk_ref, q_ref, kt_ref, v_ref = refs
    d = D_MODEL
    if classes_per_step == 1:
        h = h_ref[...]
    else:
        h = jnp.concatenate([h_ref[:, c * d:(c + 1) * d] for c in range(classes_per_step)], axis=0)
    tm = h.shape[0]
    lane = lax.broadcasted_iota(jnp.int32, (tm, LANES), 1)
    low = lane < HEAD_DIM

    q = jnp.dot(h, wq_ref[...], preferred_element_type=F32)
    for p in range(N_PAIRS):
        xq = q[:, p * LANES:(p + 1) * LANES]
        x2 = xq * xq
        s_lo = jnp.sum(jnp.where(low, x2, 0.0), axis=-1, keepdims=True)
        s_hi = jnp.sum(jnp.where(low, 0.0, x2), axis=-1, keepdims=True)
        inv = lax.rsqrt(jnp.where(low, s_lo, s_hi) * (1.0 / HEAD_DIM) + NORM_EPS)
        q_ref[p] = (xq * inv * gq_ref[...]).astype(BF16)

    kt = lax.dot_general(wkt_ref[...], h, (((1,), (1,)), ((), ())), preferred_element_type=F32)
    for tb in range(tm // BLOCK):
        blk = kt[:, tb * BLOCK:(tb + 1) * BLOCK].reshape(N_HEADS, HEAD_DIM, BLOCK)
        ss = jnp.sum(blk * blk, axis=1, keepdims=True)
        kn = blk * lax.rsqrt(ss * (1.0 / HEAD_DIM) + NORM_EPS) * gk_ref[...]
        kn = kn.reshape(N_PAIRS, 2 * HEAD_DIM, BLOCK).astype(BF16)
        for p in range(N_PAIRS):
            kt_ref[p, tb] = kn[p]

    v = jnp.dot(h, wv_ref[...], preferred_element_type=F32).astype(BF16)
    for p in range(N_PAIRS):
        v_ref[p] = v[:, p * LANES:(p + 1) * LANES]
    if with_z:
        z = jnp.dot(h, wz_ref[...], preferred_element_type=F32).astype(BF16)
        for p in range(N_PAIRS):
            z_ref[p] = z[:, p * LANES:(p + 1) * LANES]


def _project_group(h1, g, dilation, wq, wkt, wv, gq, gk, wz):
    nb, s, d = h1.shape
    length = s // dilation
    tm = TOK_TILE
    steps = s // tm
    if length >= tm:
        cps = 1
        tiles_per_class = length // tm
        hv = h1.reshape(nb, length, dilation * d)
        h_spec = pl.BlockSpec((None, tm, d), lambda b, t: (b, t % tiles_per_class, t // tiles_per_class))
    else:
        cps = tm // length
        hv = h1.reshape(nb, length, dilation * d)
        h_spec = pl.BlockSpec((None, length, cps * d), lambda b, t: (b, 0, t))
    const = lambda shape: pl.BlockSpec(shape, lambda b, t: (0,) * len(shape))
    with_z = wz is not None
    in_specs = [h_spec, const((d, d)), const((d, d)), const((d, d)), const((1, LANES)),
                const((N_HEADS, HEAD_DIM, BLOCK))]
    args = [hv, wq, wkt, wv, gq, gk]
    pm_spec = pl.BlockSpec((None, N_PAIRS, tm, LANES), lambda b, t: (b, 0, t, 0))
    pm_shape = jax.ShapeDtypeStruct((nb, N_PAIRS, s, LANES), BF16)
    out_specs = [pm_spec,
                 pl.BlockSpec((None, N_PAIRS, tm // BLOCK, LANES, BLOCK), lambda b, t: (b, 0, t, 0, 0)),
                 pm_spec]
    out_shape = [pm_shape, jax.ShapeDtypeStruct((nb, N_PAIRS, s // BLOCK, LANES, BLOCK), BF16), pm_shape]
    if with_z:
        in_specs.append(const((d, d)))
        args.append(wz)
        out_specs.append(pm_spec)
        out_shape.append(pm_shape)
    return pl.pallas_call(
        functools.partial(_proj_kernel, classes_per_step=cps, with_z=with_z),
        grid=(nb, steps),
        in_specs=in_specs,
        out_specs=out_specs,
        out_shape=out_shape,
        compiler_params=_cparams(("arbitrary", "arbitrary")),
        name=f"proj_group{g}",
    )(*args)


def _alibi_slopes():
    return np.exp2(-8.0 * np.arange(1, N_HEADS + 1, dtype=np.float64) / N_HEADS)


def _bias_tables(dilation):
    qi = np.arange(BLOCK)[:, None]
    kj = np.arange(2 * BLOCK)[None, :]
    steps = qi + BLOCK - kj
    n_steps = BLOCK
    valid = (steps >= 0) & (steps <= n_steps)
    valid_first = valid & (kj >= BLOCK)
    dist = (steps * dilation).astype(np.float64)
    slopes = _alibi_slopes()
    gen = np.empty((2, N_PAIRS, 2 * BLOCK, 2 * BLOCK), np.float32)
    for p in range(N_PAIRS):
        for half in range(2):
            sl = slopes[2 * p + half]
            rows = slice(half * BLOCK, (half + 1) * BLOCK)
            gen[0, p, rows] = np.where(valid, -sl * dist, NEG_INF)
            gen[1, p, rows] = np.where(valid_first, -sl * dist, NEG_INF)
    first = np.ascontiguousarray(gen[1, :, :, BLOCK:])
    return gen, first


def _attn_kernel(*refs, mode):
    if mode == "chain":
        q_ref, kt_ref, v_ref, ktp_ref, vp_ref, bias_ref, o_ref, st_ref = refs
    elif mode == "class4":
        q_ref, kt_ref, v_ref, bias_ref, bfirst_ref, o_ref, st_ref = refs
    else:
        q_ref, kt_ref, v_ref, bfirst_ref, o_ref, st_ref = refs

    lane = lax.broadcasted_iota(jnp.int32, (BLOCK, LANES), 1)
    low = lane < HEAD_DIM
    ones_cols = jnp.ones((BLOCK, LANES), BF16)
    is_start = (pl.program_id(1) == 0).astype(jnp.int32)

    for bi in range(Q_BLOCKS):
        rows = slice(bi * BLOCK, (bi + 1) * BLOCK)

        def pair_body(p, stats, bi=bi, rows=rows):
            qb = q_ref[p, rows, :]
            zero = jnp.zeros_like(qb)
            q2 = jnp.concatenate([jnp.where(low, qb, zero), jnp.where(low, zero, qb)], axis=0)
            k_cur = kt_ref[p, bi]
            v_cur = v_ref[p, rows, :]
            first = mode == "single" or (mode == "class4" and bi == 0)
            if first:
                kk = k_cur
                vv = jnp.concatenate([v_cur, ones_cols], axis=1)
                bias = bfirst_ref[p]
            else:
                if bi > 0:
                    k_prev = kt_ref[p, bi - 1]
                    v_prev = v_ref[p, (bi - 1) * BLOCK:bi * BLOCK, :]
                    bias = bias_ref[0, p]
                else:
                    k_prev = ktp_ref[p, 0]
                    v_prev = vp_ref[p]
                    bias = bias_ref[is_start, p]
                kk = jnp.concatenate([k_prev, k_cur], axis=1)
                vv = jnp.concatenate(
                    [jnp.concatenate([v_prev, v_cur], axis=0),
                     jnp.concatenate([ones_cols, ones_cols], axis=0)], axis=1)
            s = jnp.dot(q2, kk, preferred_element_type=F32) + bias
            m = jnp.max(s, axis=-1, keepdims=True)
            pe = jnp.exp(s - m).astype(BF16)
            r = jnp.dot(pe, vv, preferred_element_type=F32)
            o = jnp.where(low, r[0:BLOCK, 0:LANES], r[BLOCK:2 * BLOCK, 0:LANES]).astype(BF16)
            if mode == "single":
                o_ref[p, :, bi * LANES:(bi + 1) * LANES] = o
            else:
                o_ref[p, rows, :] = o
            m_a = m[0:BLOCK]
            m_b = m[BLOCK:2 * BLOCK]
            l_a = r[0:BLOCK, LANES:2 * LANES]
            l_b = r[BLOCK:2 * BLOCK, LANES:2 * LANES]
            stats = jnp.where(lane == 2 * p, m_a, stats)
            stats = jnp.where(lane == 2 * p + 1, m_b, stats)
            stats = jnp.where(lane == N_HEADS + 2 * p, l_a, stats)
            stats = jnp.where(lane == N_HEADS + 2 * p + 1, l_b, stats)
            return stats

        stats = lax.fori_loop(0, N_PAIRS, pair_body, jnp.ones((BLOCK, LANES), F32))
        if mode == "single":
            st_ref[:, bi * LANES:(bi + 1) * LANES] = stats
        else:
            st_ref[rows, :] = stats


def _attention_group(g, dilation, q, kt, v):
    nb, _, s, _ = q.shape
    length = s // dilation
    tm = TOK_TILE
    steps = s // tm
    gen, first = _bias_tables(dilation)
    const = lambda shape: pl.BlockSpec(shape, lambda b, t: (0,) * len(shape))
    q_spec = pl.BlockSpec((None, N_PAIRS, tm, LANES), lambda b, t: (b, 0, t, 0))
    kt_spec = pl.BlockSpec((None, N_PAIRS, Q_BLOCKS, LANES, BLOCK), lambda b, t: (b, 0, t, 0, 0))
    in_specs = [q_spec, kt_spec, q_spec]
    args = [q, kt, v]
    if length > tm:
        mode = "chain"
        prev = lambda t: jnp.maximum(t * Q_BLOCKS - 1, 0)
        in_specs += [
            pl.BlockSpec((None, N_PAIRS, 1, LANES, BLOCK), lambda b, t: (b, 0, prev(t), 0, 0)),
            pl.BlockSpec((None, N_PAIRS, BLOCK, LANES), lambda b, t: (b, 0, prev(t), 0)),
            const(gen.shape),
        ]
        args += [kt, v, jnp.asarray(gen)]
    elif length == tm:
        mode = "class4"
        in_specs += [const(gen[0:1].shape), const(first.shape)]
        args += [jnp.asarray(gen[0:1]), jnp.asarray(first)]
    else:
        assert length == BLOCK
        mode = "single"
        in_specs += [const(first.shape)]
        args += [jnp.asarray(first)]

    if mode == "chain":
        o_view, st_view = (nb, N_PAIRS, s, LANES), (nb, s, LANES)
        o_spec = pl.BlockSpec((None, N_PAIRS, tm, LANES), lambda b, t: (b, 0, t, 0))
        st_spec = pl.BlockSpec((None, tm, LANES), lambda b, t: (b, t, 0))
    elif mode == "class4":
        o_view, st_view = (nb, N_PAIRS, length, dilation * LANES), (nb, length, dilation * LANES)
        o_spec = pl.BlockSpec((None, N_PAIRS, tm, LANES), lambda b, t: (b, 0, 0, t))
        st_spec = pl.BlockSpec((None, tm, LANES), lambda b, t: (b, 0, t))
    else:
        o_view, st_view = (nb, N_PAIRS, length, dilation * LANES), (nb, length, dilation * LANES)
        o_spec = pl.BlockSpec((None, N_PAIRS, BLOCK, Q_BLOCKS * LANES), lambda b, t: (b, 0, 0, t))
        st_spec = pl.BlockSpec((None, BLOCK, Q_BLOCKS * LANES), lambda b, t: (b, 0, t))
    o, st = pl.pallas_call(
        functools.partial(_attn_kernel, mode=mode),
        grid=(nb, steps),
        in_specs=in_specs,
        out_specs=[o_spec, st_spec],
        out_shape=[jax.ShapeDtypeStruct(o_view, BF16), jax.ShapeDtypeStruct(st_view, F32)],
        compiler_params=_cparams(("arbitrary", "arbitrary")),
        name=f"attn_group{g}",
    )(*args)
    return o.reshape(nb, N_PAIRS, s, LANES), st.reshape(nb, s, LANES)


def _merge_kernel(o1_ref, o2_ref, o3_ref, s1_ref, s2_ref, s3_ref, z_ref, x_ref, mod_ref, e_ref, wout_ref,
                  out_ref):
    stats = [s1_ref[...], s2_ref[...], s3_ref[...]]
    dens = [pltpu.roll(st, LANES - N_HEADS, axis=1) for st in stats]
    m = jnp.maximum(jnp.maximum(stats[0], stats[1]), stats[2])
    es = [jnp.exp(st - m) for st in stats]
    den = es[0] * dens[0] + es[1] * dens[1] + es[2] * dens[2]
    inv = 1.0 / den
    lane = lax.broadcasted_iota(jnp.int32, m.shape, 1)
    o_refs = [o1_ref, o2_ref, o3_ref]
    acc = None
    for g in range(3):
        w = jnp.where(lane < N_HEADS, es[g] * inv, 0.0)
        w_hi = w.astype(BF16)
        w_lo = (w - w_hi.astype(F32)).astype(BF16)
        wx = jnp.dot(jnp.concatenate([w_hi, w_lo], axis=1), e_ref[...], preferred_element_type=F32)
        og = jnp.concatenate([o_refs[g][p] for p in range(N_PAIRS)], axis=1).astype(F32)
        acc = wx * og if acc is None else acc + wx * og
    z = jnp.concatenate([z_ref[p] for p in range(N_PAIRS)], axis=1).astype(F32)
    a = (acc * _silu(z)).astype(BF16)
    y = jnp.dot(a, wout_ref[...], preferred_element_type=F32)
    out_ref[...] = x_ref[...] + mod_ref[2:3, :] * y


def _merge(os_, stats, z, x1, mod, w_out):
    nb, s, d = x1.shape
    tm = TOK_TILE
    expand = np.zeros((2 * LANES, d), np.float32)
    for hd in range(N_HEADS):
        expand[hd, hd * HEAD_DIM:(hd + 1) * HEAD_DIM] = 1.0
        expand[LANES + hd, hd * HEAD_DIM:(hd + 1) * HEAD_DIM] = 1.0
    pm_spec = pl.BlockSpec((None, N_PAIRS, tm, LANES), lambda b, t: (b, 0, t, 0))
    st_spec = pl.BlockSpec((None, tm, LANES), lambda b, t: (b, t, 0))
    x_spec = pl.BlockSpec((None, tm, d), lambda b, t: (b, t, 0))
    const = lambda shape: pl.BlockSpec(shape, lambda b, t: (0,) * len(shape))
    return pl.pallas_call(
        _merge_kernel,
        grid=(nb, s // tm),
        in_specs=[pm_spec, pm_spec, pm_spec, st_spec, st_spec, st_spec, pm_spec, x_spec,
                  pl.BlockSpec((None, None, 3, d), lambda b, t: (1, b, 0, 0)),
                  const((2 * LANES, d)), const((d, d))],
        out_specs=x_spec,
        out_shape=jax.ShapeDtypeStruct((nb, s, d), F32),
        compiler_params=_cparams(("arbitrary", "arbitrary")),
        name="merge_out_proj",
    )(*os_, *stats, z, x1, mod, jnp.asarray(expand, BF16), w_out.astype(BF16))


def kernel(x, c, norm_g, ada_w, ada_b, a_w_in, a_conv_w, a_conv_b, a_ln_g, a_ln_b, a_w_out, b_w_in, b_q_norm,
           b_k_norm, b_w_out):
    d = D_MODEL
    mod = _ada_modulation(c, ada_w, ada_b)
    x1, h1 = _layer0(x, mod, norm_g, a_w_in[0], a_conv_w[0], a_conv_b[0], a_ln_g[0], a_ln_b[0], a_w_out[0])

    w_b = b_w_in[0]
    wz = w_b[:, 9 * d:10 * d].astype(BF16)
    scale = HEAD_DIM ** -0.5
    os_, stats, z = [], [], None
    for g, (_, dilation) in enumerate(DILATED_GROUPS):
        base = 3 * g * d
        wq = w_b[:, base:base + d].astype(BF16)
        wkt = w_b[:, base + d:base + 2 * d].T.astype(BF16)
        wv = w_b[:, base + 2 * d:base + 3 * d].astype(BF16)
        gq = jnp.tile(b_q_norm[0, g] * scale, 2).reshape(1, LANES)
        gk = jnp.broadcast_to(b_k_norm[0, g][None, :, None], (N_HEADS, HEAD_DIM, BLOCK))
        outs = _project_group(h1, g, dilation, wq, wkt, wv, gq, gk, wz if g == 0 else None)
        if g == 0:
            q, kt, v, z = outs
        else:
            q, kt, v = outs
        o, st = _attention_group(g, dilation, q, kt, v)
        os_.append(o)
        stats.append(st)
    return _merge(os_, stats, z, x1, mod, b_w_out[0])
```

```python
import functools

import jax
import jax.numpy as jnp
import numpy as np
from jax import lax
from jax.experimental import pallas as pl
from jax.experimental.pallas import tpu as pltpu

F32 = jnp.float32
BF16 = jnp.bfloat16

D_MODEL = 1024
CONV_WIDTH = 31
HEAD_DIM = 64
N_HEADS = 16
N_PAIRS = N_HEADS // 2
DILATED_GROUPS = ((128, 1), (512, 4), (2048, 16))
BLOCK = 128
NORM_EPS = 1e-6
NEG_INF = -1e30
LANES = 128
VMEM_LIMIT = 56 * 1024 * 1024

TOK_TILE = 512
Q_BLOCKS = TOK_TILE // BLOCK
L0_TILE = 256
HALO = 32
CONV_ROWS = 128


def _sigmoid(x):
    return 1.0 / (1.0 + jnp.exp(-x))


def _silu(x):
    return x * _sigmoid(x)


def _cparams(sem):
    return pltpu.CompilerParams(dimension_semantics=sem, vmem_limit_bytes=VMEM_LIMIT)


def _ada_kernel(c_ref, w_ref, b_ref, o_ref):
    c = c_ref[...]
    sc = _silu(c).astype(BF16)
    o_ref[...] = jnp.dot(sc, w_ref[...].astype(BF16), preferred_element_type=F32) + b_ref[...]


def _ada_modulation(c, ada_w, ada_b):
    depth, d, d3 = ada_w.shape
    nb = c.shape[0]
    cols = 768
    out = pl.pallas_call(
        _ada_kernel,
        grid=(depth, d3 // cols),
        in_specs=[
            pl.BlockSpec((nb, d), lambda l, j: (0, 0)),
            pl.BlockSpec((None, d, cols), lambda l, j: (l, 0, j)),
            pl.BlockSpec((None, 1, cols), lambda l, j: (l, 0, j)),
        ],
        out_specs=pl.BlockSpec((None, nb, cols), lambda l, j: (l, 0, j)),
        out_shape=jax.ShapeDtypeStruct((depth, nb, d3), F32),
        compiler_params=_cparams(("arbitrary", "arbitrary")),
        name="ada_modulation",
    )(c, ada_w, ada_b.reshape(depth, 1, d3))
    return out.reshape(depth, nb, 3, d)


def _layer0_kernel(x_ref, mod0_ref, mod1_ref, g0_ref, g1_ref, win_ref, cw_ref, cb_ref, lg_ref, lb_ref,
                   wout_ref, x1_ref, h1_ref, ubuf, cbuf):
    ts = x_ref.shape[0]
    d = x_ref.shape[1]

    @pl.when(pl.program_id(1) == 0)
    def _():
        ubuf[0:HALO, :] = jnp.zeros((HALO, d), F32)

    x = x_ref[...]
    shift, scale, gate = mod0_ref[0:1, :], mod0_ref[1:2, :], mod0_ref[2:3, :]
    ms = jnp.mean(x * x, axis=-1, keepdims=True)
    h = (x * lax.rsqrt(ms + NORM_EPS) * g0_ref[...]) * (1.0 + scale) + shift
    proj = jnp.dot(h.astype(BF16), win_ref[...], preferred_element_type=F32)
    u = proj[:, 0:d] * _sigmoid(proj[:, d:2 * d])
    sz = _silu(proj[:, 2 * d:3 * d])
    ubuf[HALO:HALO + ts, :] = u

    off = HALO - (CONV_WIDTH - 1)
    for ls in range(d // LANES):
        lsl = slice(ls * LANES, (ls + 1) * LANES)
        for rc in range(ts // CONV_ROWS):
            r0 = rc * CONV_ROWS
            acc = jnp.broadcast_to(cb_ref[0:1, lsl], (CONV_ROWS, LANES))
            for j in range(CONV_WIDTH):
                acc = acc + cw_ref[j:j + 1, lsl] * ubuf[r0 + off + j:r0 + off + j + CONV_ROWS, lsl]
            cbuf[r0:r0 + CONV_ROWS, lsl] = acc
    ubuf[0:HALO, :] = ubuf[ts:ts + HALO, :]

    cv = cbuf[...]
    mu = jnp.mean(cv, axis=-1, keepdims=True)
    xc = cv - mu
    var = jnp.mean(xc * xc, axis=-1, keepdims=True)
    y = xc * lax.rsqrt(var + NORM_EPS) * lg_ref[...] + lb_ref[...]
    a = (_silu(y) * sz).astype(BF16)
    out = jnp.dot(a, wout_ref[...], preferred_element_type=F32)
    x1 = x + gate * out
    x1_ref[...] = x1

    shift1, scale1 = mod1_ref[0:1, :], mod1_ref[1:2, :]
    ms1 = jnp.mean(x1 * x1, axis=-1, keepdims=True)
    h1 = (x1 * lax.rsqrt(ms1 + NORM_EPS) * g1_ref[...]) * (1.0 + scale1) + shift1
    h1_ref[...] = h1.astype(BF16)


def _layer0(x, mod, norm_g, w_in, conv_w, conv_b, ln_g, ln_b, w_out):
    nb, s, d = x.shape
    ts = L0_TILE
    row = lambda a: a.reshape(1, d)
    const = lambda shape: pl.BlockSpec(shape, lambda b, t: (0,) * len(shape))
    return pl.pallas_call(
        _layer0_kernel,
        grid=(nb, s // ts),
        in_specs=[
            pl.BlockSpec((None, ts, d), lambda b, t: (b, t, 0)),
            pl.BlockSpec((None, None, 3, d), lambda b, t: (0, b, 0, 0)),
            pl.BlockSpec((None, None, 3, d), lambda b, t: (1, b, 0, 0)),
            const((1, d)), const((1, d)),
            const((d, 3 * d)),
            const((CONV_WIDTH, d)), const((1, d)), const((1, d)), const((1, d)),
            const((d, d)),
        ],
        out_specs=[
            pl.BlockSpec((None, ts, d), lambda b, t: (b, t, 0)),
            pl.BlockSpec((None, ts, d), lambda b, t: (b, t, 0)),
        ],
        out_shape=[jax.ShapeDtypeStruct((nb, s, d), F32), jax.ShapeDtypeStruct((nb, s, d), BF16)],
        scratch_shapes=[pltpu.VMEM((HALO + ts, d), F32), pltpu.VMEM((ts, d), F32)],
        compiler_params=_cparams(("arbitrary", "arbitrary")),
        name="layer0_conv_mixer",
    )(x, mod, mod, row(norm_g[0]), row(norm_g[1]), w_in.astype(BF16), conv_w, row(conv_b), row(ln_g),
      row(ln_b), w_out.astype(BF16))


def _proj_kernel(*refs, classes_per_step, with_z):
    if with_z:
        h_ref, wq_ref, wkt_ref, wv_ref, gq_ref, gk_ref, wz_ref, q_ref, kt_ref, v_ref, z_ref = refs
    else:
        h_ref, wq_ref, wkt_ref, wv_ref, gq_ref, gk_ref, q_ref, kt_ref, v_ref = refs
    d = D_MODEL
    if classes_per_step == 1:
        h = h_ref[...]
    else:
        h = jnp.concatenate([h_ref[:, c * d:(c + 1) * d] for c in range(classes_per_step)], axis=0)
    tm = h.shape[0]
    lane = lax.broadcasted_iota(jnp.int32, (tm, LANES), 1)
    low = lane < HEAD_DIM

    q = jnp.dot(h, wq_ref[...], preferred_element_type=F32)
    for p in range(N_PAIRS):
        xq = q[:, p * LANES:(p + 1) * LANES]
        x2 = xq * xq
        s_lo = jnp.sum(jnp.where(low, x2, 0.0), axis=-1, keepdims=True)
        s_hi = jnp.sum(jnp.where(low, 0.0, x2), axis=-1, keepdims=True)
        inv = lax.rsqrt(jnp.where(low, s_lo, s_hi) * (1.0 / HEAD_DIM) + NORM_EPS)
        q_ref[p] = (xq * inv * gq_ref[...]).astype(BF16)

    kt = lax.dot_general(wkt_ref[...], h, (((1,), (1,)), ((), ())), preferred_element_type=F32)
    for tb in range(tm // BLOCK):
        blk = kt[:, tb * BLOCK:(tb + 1) * BLOCK].reshape(N_HEADS, HEAD_DIM, BLOCK)
        ss = jnp.sum(blk * blk, axis=1, keepdims=True)
        kn = blk * lax.rsqrt(ss * (1.0 / HEAD_DIM) + NORM_EPS) * gk_ref[...]
        kn = kn.reshape(N_PAIRS, 2 * HEAD_DIM, BLOCK).astype(BF16)
        for p in range(N_PAIRS):
            kt_ref[p, tb] = kn[p]

    v = jnp.dot(h, wv_ref[...], preferred_element_type=F32).astype(BF16)
    for p in range(N_PAIRS):
        v_ref[p] = v[:, p * LANES:(p + 1) * LANES]
    if with_z:
        z = jnp.dot(h, wz_ref[...], preferred_element_type=F32).astype(BF16)
        for p in range(N_PAIRS):
            z_ref[p] = z[:, p * LANES:(p + 1) * LANES]


def _project_group(h1, g, dilation, wq, wkt, wv, gq, gk, wz):
    nb, s, d = h1.shape
    length = s // dilation
    tm = TOK_TILE
    steps = s // tm
    if length >= tm:
        cps = 1
        tiles_per_class = length // tm
        hv = h1.reshape(nb, length, dilation * d)
        h_spec = pl.BlockSpec((None, tm, d), lambda b, t: (b, t % tiles_per_class, t // tiles_per_class))
    else:
        cps = tm // length
        hv = h1.reshape(nb, length, dilation * d)
        h_spec = pl.BlockSpec((None, length, cps * d), lambda b, t: (b, 0, t))
    const = lambda shape: pl.BlockSpec(shape, lambda b, t: (0,) * len(shape))
    with_z = wz is not None
    in_specs = [h_spec, const((d, d)), const((d, d)), const((d, d)), const((1, LANES)),
                const((N_HEADS, HEAD_DIM, BLOCK))]
    args = [hv, wq, wkt, wv, gq, gk]
    pm_spec = pl.BlockSpec((None, N_PAIRS, tm, LANES), lambda b, t: (b, 0, t, 0))
    pm_shape = jax.ShapeDtypeStruct((nb, N_PAIRS, s, LANES), BF16)
    out_specs = [pm_spec,
                 pl.BlockSpec((None, N_PAIRS, tm // BLOCK, LANES, BLOCK), lambda b, t: (b, 0, t, 0, 0)),
                 pm_spec]
    out_shape = [pm_shape, jax.ShapeDtypeStruct((nb, N_PAIRS, s // BLOCK, LANES, BLOCK), BF16), pm_shape]
    if with_z:
        in_specs.append(const((d, d)))
        args.append(wz)
        out_specs.append(pm_spec)
        out_shape.append(pm_shape)
    return pl.pallas_call(
        functools.partial(_proj_kernel, classes_per_step=cps, with_z=with_z),
        grid=(nb, steps),
        in_specs=in_specs,
        out_specs=out_specs,
        out_shape=out_shape,
        compiler_params=_cparams(("arbitrary", "arbitrary")),
        name=f"proj_group{g}",
    )(*args)


def _alibi_slopes():
    return np.exp2(-8.0 * np.arange(1, N_HEADS + 1, dtype=np.float64) / N_HEADS)


def _bias_tables(dilation):
    qi = np.arange(BLOCK)[:, None]
    kj = np.arange(2 * BLOCK)[None, :]
    steps = qi + BLOCK - kj
    n_steps = BLOCK
    valid = (steps >= 0) & (steps <= n_steps)
    valid_first = valid & (kj >= BLOCK)
    dist = (steps * dilation).astype(np.float64)
    slopes = _alibi_slopes()
    gen = np.empty((2, N_PAIRS, 2 * BLOCK, 2 * BLOCK), np.float32)
    for p in range(N_PAIRS):
        for half in range(2):
            sl = slopes[2 * p + half]
            rows = slice(half * BLOCK, (half + 1) * BLOCK)
            gen[0, p, rows] = np.where(valid, -sl * dist, NEG_INF)
            gen[1, p, rows] = np.where(valid_first, -sl * dist, NEG_INF)
    first = np.ascontiguousarray(gen[1, :, :, BLOCK:])
    return gen, first


def _attn_kernel(*refs, mode):
    if mode == "chain":
        q_ref, kt_ref, v_ref, ktp_ref, vp_ref, bias_ref, o_ref, st_ref = refs
    elif mode == "class4":
        q_ref, kt_ref, v_ref, bias_ref, bfirst_ref, o_ref, st_ref = refs
    else:
        q_ref, kt_ref, v_ref, bfirst_ref, o_ref, st_ref = refs

    lane = lax.broadcasted_iota(jnp.int32, (BLOCK, LANES), 1)
    low = lane < HEAD_DIM
    ones_cols = jnp.ones((BLOCK, LANES), BF16)
    is_start = (pl.program_id(1) == 0).astype(jnp.int32)

    for bi in range(Q_BLOCKS):
        rows = slice(bi * BLOCK, (bi + 1) * BLOCK)

        def pair_body(p, stats, bi=bi, rows=rows):
            qb = q_ref[p, rows, :]
            zero = jnp.zeros_like(qb)
            q2 = jnp.concatenate([jnp.where(low, qb, zero), jnp.where(low, zero, qb)], axis=0)
            k_cur = kt_ref[p, bi]
            v_cur = v_ref[p, rows, :]
            first = mode == "single" or (mode == "class4" and bi == 0)
            if first:
                kk = k_cur
                vv = jnp.concatenate([v_cur, ones_cols], axis=1)
                bias = bfirst_ref[p]
            else:
                if bi > 0:
                    k_prev = kt_ref[p, bi - 1]
                    v_prev = v_ref[p, (bi - 1) * BLOCK:bi * BLOCK, :]
                    bias = bias_ref[0, p]
                else:
                    k_prev = ktp_ref[p, 0]
                    v_prev = vp_ref[p]
                    bias = bias_ref[is_start, p]
                kk = jnp.concatenate([k_prev, k_cur], axis=1)
                vv = jnp.concatenate(
                    [jnp.concatenate([v_prev, v_cur], axis=0),
                     jnp.concatenate([ones_cols, ones_cols], axis=0)], axis=1)
            s = jnp.dot(q2, kk, preferred_element_type=F32) + bias
            m = jnp.max(s, axis=-1, keepdims=True)
            pe = jnp.exp(s - m).astype(BF16)
            r = jnp.dot(pe, vv, preferred_element_type=F32)
            o = jnp.where(low, r[0:BLOCK, 0:LANES], r[BLOCK:2 * BLOCK, 0:LANES]).astype(BF16)
            if mode == "single":
                o_ref[p, :, bi * LANES:(bi + 1) * LANES] = o
            else:
                o_ref[p, rows, :] = o
            m_a = m[0:BLOCK]
            m_b = m[BLOCK:2 * BLOCK]
            l_a = r[0:BLOCK, LANES:2 * LANES]
            l_b = r[BLOCK:2 * BLOCK, LANES:2 * LANES]
            stats = jnp.where(lane == 2 * p, m_a, stats)
            stats = jnp.where(lane == 2 * p + 1, m_b, stats)
            stats = jnp.where(lane == N_HEADS + 2 * p, l_a, stats)
            stats = jnp.where(lane == N_HEADS + 2 * p + 1, l_b, stats)
            return stats

        stats = jnp.ones((BLOCK, LANES), F32)
        for p in range(N_PAIRS):
            stats = pair_body(p, stats)
        if mode == "single":
            st_ref[:, bi * LANES:(bi + 1) * LANES] = stats
        else:
            st_ref[rows, :] = stats


def _attention_group(g, dilation, q, kt, v):
    nb, _, s, _ = q.shape
    length = s // dilation
    tm = TOK_TILE
    steps = s // tm
    gen, first = _bias_tables(dilation)
    const = lambda shape: pl.BlockSpec(shape, lambda b, t: (0,) * len(shape))
    q_spec = pl.BlockSpec((None, N_PAIRS, tm, LANES), lambda b, t: (b, 0, t, 0))
    kt_spec = pl.BlockSpec((None, N_PAIRS, Q_BLOCKS, LANES, BLOCK), lambda b, t: (b, 0, t, 0, 0))
    in_specs = [q_spec, kt_spec, q_spec]
    args = [q, kt, v]
    if length > tm:
        mode = "chain"
        prev = lambda t: jnp.maximum(t * Q_BLOCKS - 1, 0)
        in_specs += [
            pl.BlockSpec((None, N_PAIRS, 1, LANES, BLOCK), lambda b, t: (b, 0, prev(t), 0, 0)),
            pl.BlockSpec((None, N_PAIRS, BLOCK, LANES), lambda b, t: (b, 0, prev(t), 0)),
            const(gen.shape),
        ]
        args += [kt, v, jnp.asarray(gen)]
    elif length == tm:
        mode = "class4"
        in_specs += [const(gen[0:1].shape), const(first.shape)]
        args += [jnp.asarray(gen[0:1]), jnp.asarray(first)]
    else:
        assert length == BLOCK
        mode = "single"
        in_specs += [const(first.shape)]
        args += [jnp.asarray(first)]

    if mode == "chain":
        o_view, st_view = (nb, N_PAIRS, s, LANES), (nb, s, LANES)
        o_spec = pl.BlockSpec((None, N_PAIRS, tm, LANES), lambda b, t: (b, 0, t, 0))
        st_spec = pl.BlockSpec((None, tm, LANES), lambda b, t: (b, t, 0))
    elif mode == "class4":
        o_view, st_view = (nb, N_PAIRS, length, dilation * LANES), (nb, length, dilation * LANES)
        o_spec = pl.BlockSpec((None, N_PAIRS, tm, LANES), lambda b, t: (b, 0, 0, t))
        st_spec = pl.BlockSpec((None, tm, LANES), lambda b, t: (b, 0, t))
    else:
        o_view, st_view = (nb, N_PAIRS, length, dilation * LANES), (nb, length, dilation * LANES)
        o_spec = pl.BlockSpec((None, N_PAIRS, BLOCK, Q_BLOCKS * LANES), lambda b, t: (b, 0, 0, t))
        st_spec = pl.BlockSpec((None, BLOCK, Q_BLOCKS * LANES), lambda b, t: (b, 0, t))
    o, st = pl.pallas_call(
        functools.partial(_attn_kernel, mode=mode),
        grid=(nb, steps),
        in_specs=in_specs,
        out_specs=[o_spec, st_spec],
        out_shape=[jax.ShapeDtypeStruct(o_view, BF16), jax.ShapeDtypeStruct(st_view, F32)],
        compiler_params=_cparams(("arbitrary", "arbitrary")),
        name=f"attn_group{g}",
    )(*args)
    return o.reshape(nb, N_PAIRS, s, LANES), st.reshape(nb, s, LANES)


def _merge_kernel(o1_ref, o2_ref, o3_ref, s1_ref, s2_ref, s3_ref, z_ref, x_ref, mod_ref, e_ref, wout_ref,
                  out_ref):
    stats = [s1_ref[...], s2_ref[...], s3_ref[...]]
    dens = [pltpu.roll(st, LANES - N_HEADS, axis=1) for st in stats]
    m = jnp.maximum(jnp.maximum(stats[0], stats[1]), stats[2])
    es = [jnp.exp(st - m) for st in stats]
    den = es[0] * dens[0] + es[1] * dens[1] + es[2] * dens[2]
    inv = 1.0 / den
    lane = lax.broadcasted_iota(jnp.int32, m.shape, 1)
    o_refs = [o1_ref, o2_ref, o3_ref]
    acc = None
    for g in range(3):
        w = jnp.where(lane < N_HEADS, es[g] * inv, 0.0)
        w_hi = w.astype(BF16)
        w_lo = (w - w_hi.astype(F32)).astype(BF16)
        wx = jnp.dot(jnp.concatenate([w_hi, w_lo], axis=1), e_ref[...], preferred_element_type=F32)
        og = jnp.concatenate([o_refs[g][p] for p in range(N_PAIRS)], axis=1).astype(F32)
        acc = wx * og if acc is None else acc + wx * og
    z = jnp.concatenate([z_ref[p] for p in range(N_PAIRS)], axis=1).astype(F32)
    a = (acc * _silu(z)).astype(BF16)
    y = jnp.dot(a, wout_ref[...], preferred_element_type=F32)
    out_ref[...] = x_ref[...] + mod_ref[2:3, :] * y


def _merge(os_, stats, z, x1, mod, w_out):
    nb, s, d = x1.shape
    tm = TOK_TILE
    expand = np.zeros((2 * LANES, d), np.float32)
    for hd in range(N_HEADS):
        expand[hd, hd * HEAD_DIM:(hd + 1) * HEAD_DIM] = 1.0
        expand[LANES + hd, hd * HEAD_DIM:(hd + 1) * HEAD_DIM] = 1.0
    pm_spec = pl.BlockSpec((None, N_PAIRS, tm, LANES), lambda b, t: (b, 0, t, 0))
    st_spec = pl.BlockSpec((None, tm, LANES), lambda b, t: (b, t, 0))
    x_spec = pl.BlockSpec((None, tm, d), lambda b, t: (b, t, 0))
    const = lambda shape: pl.BlockSpec(shape, lambda b, t: (0,) * len(shape))
    return pl.pallas_call(
        _merge_kernel,
        grid=(nb, s // tm),
        in_specs=[pm_spec, pm_spec, pm_spec, st_spec, st_spec, st_spec, pm_spec, x_spec,
                  pl.BlockSpec((None, None, 3, d), lambda b, t: (1, b, 0, 0)),
                  const((2 * LANES, d)), const((d, d))],
        out_specs=x_spec,
        out_shape=jax.ShapeDtypeStruct((nb, s, d), F32),
        compiler_params=_cparams(("arbitrary", "arbitrary")),
        name="merge_out_proj",
    )(*os_, *stats, z, x1, mod, jnp.asarray(expand, BF16), w_out.astype(BF16))


def kernel(x, c, norm_g, ada_w, ada_b, a_w_in, a_conv_w, a_conv_b, a_ln_g, a_ln_b, a_w_out, b_w_in, b_q_norm,
           b_k_norm, b_w_out):
    d = D_MODEL
    mod = _ada_modulation(c, ada_w, ada_b)
    x1, h1 = _layer0(x, mod, norm_g, a_w_in[0], a_conv_w[0], a_conv_b[0], a_ln_g[0], a_ln_b[0], a_w_out[0])

    w_b = b_w_in[0]
    wz = w_b[:, 9 * d:10 * d].astype(BF16)
    scale = HEAD_DIM ** -0.5
    os_, stats, z = [], [], None
    for g, (_, dilation) in enumerate(DILATED_GROUPS):
        base = 3 * g * d
        wq = w_b[:, base:base + d].astype(BF16)
        wkt = w_b[:, base + d:base + 2 * d].T.astype(BF16)
        wv = w_b[:, base + 2 * d:base + 3 * d].astype(BF16)
        gq = jnp.tile(b_q_norm[0, g] * scale, 2).reshape(1, LANES)
        gk = jnp.broadcast_to(b_k_norm[0, g][None, :, None], (N_HEADS, HEAD_DIM, BLOCK))
        outs = _project_group(h1, g, dilation, wq, wkt, wv, gq, gk, wz if g == 0 else None)
        if g == 0:
            q, kt, v, z = outs
        else:
            q, kt, v = outs
        o, st = _attention_group(g, dilation, q, kt, v)
        os_.append(o)
        stats.append(st)
    return _merge(os_, stats, z, x1, mod, b_w_out[0])
```

```python
import functools

import jax
import jax.numpy as jnp
import numpy as np
from jax import lax
from jax.experimental import pallas as pl
from jax.experimental.pallas import tpu as pltpu

F32 = jnp.float32
BF16 = jnp.bfloat16

D_MODEL = 1024
CONV_WIDTH = 31
HEAD_DIM = 64
N_HEADS = 16
N_PAIRS = N_HEADS // 2
DILATED_GROUPS = ((128, 1), (512, 4), (2048, 16))
BLOCK = 128
NORM_EPS = 1e-6
NEG_INF = -1e30
LANES = 128
SUBLANES = 8
VMEM_LIMIT = 56 * 1024 * 1024

TOK_TILE = 512
Q_BLOCKS = TOK_TILE // BLOCK
L0_TILE = 256
HALO = 32
CONV_ROWS = 128


def _sigmoid(x):
    return 1.0 / (1.0 + jnp.exp(-x))


def _silu(x):
    return x * _sigmoid(x)


def _cparams(sem):
    return pltpu.CompilerParams(dimension_semantics=sem, vmem_limit_bytes=VMEM_LIMIT)


def _ada_kernel(c_ref, w_ref, b_ref, o_ref):
    c = c_ref[...]
    sc = _silu(c).astype(BF16)
    o_ref[...] = jnp.dot(sc, w_ref[...].astype(BF16), preferred_element_type=F32) + b_ref[...]


def _ada_modulation(c, ada_w, ada_b):
    depth, d, d3 = ada_w.shape
    nb = c.shape[0]
    cols = 768
    out = pl.pallas_call(
        _ada_kernel,
        grid=(depth, d3 // cols),
        in_specs=[
            pl.BlockSpec((nb, d), lambda l, j: (0, 0)),
            pl.BlockSpec((None, d, cols), lambda l, j: (l, 0, j)),
            pl.BlockSpec((None, 1, cols), lambda l, j: (l, 0, j)),
        ],
        out_specs=pl.BlockSpec((None, nb, cols), lambda l, j: (l, 0, j)),
        out_shape=jax.ShapeDtypeStruct((depth, nb, d3), F32),
        compiler_params=_cparams(("arbitrary", "arbitrary")),
        name="ada_modulation",
    )(c, ada_w, ada_b.reshape(depth, 1, d3))
    return out.reshape(depth, nb, 3, d)


def _layer0_kernel(x_ref, mod0_ref, mod1_ref, g0_ref, g1_ref, win_ref, cw_ref, cb_ref, lg_ref, lb_ref,
                   wout_ref, x1_ref, h1_ref, h1c4_ref, h1c16_ref, ubuf, sbuf, cbuf, hstage):
    ts = x_ref.shape[0]
    d = x_ref.shape[1]

    @pl.when(pl.program_id(1) == 0)
    def _():
        ubuf[0:HALO, :] = jnp.zeros((HALO, d), F32)

    x = x_ref[...]
    shift, scale, gate = mod0_ref[0:1, :], mod0_ref[1:2, :], mod0_ref[2:3, :]
    ms = jnp.mean(x * x, axis=-1, keepdims=True)
    h = (x * lax.rsqrt(ms + NORM_EPS) * g0_ref[...]) * (1.0 + scale) + shift
    proj = jnp.dot(h.astype(BF16), win_ref[...], preferred_element_type=F32)
    u = proj[:, 0:d] * _sigmoid(proj[:, d:2 * d])
    sz = _silu(proj[:, 2 * d:3 * d])
    ubuf[HALO:HALO + ts, :] = u

    off = HALO - (CONV_WIDTH - 1)
    shifted_rows = sbuf.shape[1]
    for s in range(1, SUBLANES):
        sbuf[s - 1] = ubuf[s:s + shifted_rows, :]
    for ls in range(d // LANES):
        lsl = slice(ls * LANES, (ls + 1) * LANES)
        for rc in range(ts // CONV_ROWS):
            r0 = rc * CONV_ROWS
            acc = jnp.broadcast_to(cb_ref[0:1, lsl], (CONV_ROWS, LANES))
            for j in range(CONV_WIDTH):
                s = (off + j) % SUBLANES
                a = r0 + off + j - s
                if s == 0:
                    tap = ubuf[a:a + CONV_ROWS, lsl]
                else:
                    tap = sbuf[s - 1, a:a + CONV_ROWS, lsl]
                acc = acc + cw_ref[j:j + 1, lsl] * tap
            cbuf[r0:r0 + CONV_ROWS, lsl] = acc
    ubuf[0:HALO, :] = ubuf[ts:ts + HALO, :]

    cv = cbuf[...]
    mu = jnp.mean(cv, axis=-1, keepdims=True)
    xc = cv - mu
    var = jnp.mean(xc * xc, axis=-1, keepdims=True)
    y = xc * lax.rsqrt(var + NORM_EPS) * lg_ref[...] + lb_ref[...]
    a = (_silu(y) * sz).astype(BF16)
    out = jnp.dot(a, wout_ref[...], preferred_element_type=F32)
    x1 = x + gate * out
    x1_ref[...] = x1

    shift1, scale1 = mod1_ref[0:1, :], mod1_ref[1:2, :]
    ms1 = jnp.mean(x1 * x1, axis=-1, keepdims=True)
    h1 = (x1 * lax.rsqrt(ms1 + NORM_EPS) * g1_ref[...]) * (1.0 + scale1) + shift1
    h1_ref[...] = h1.astype(BF16)
    for ls in range(d // LANES):
        lsl = slice(ls * LANES, (ls + 1) * LANES)
        hstage[ls] = h1[:, lsl]
        for ref in (h1c4_ref, h1c16_ref):
            dil, per_class = ref.shape[0], ref.shape[1]
            for r in range(dil):
                ref[r, :, lsl] = hstage[ls, pl.ds(r, per_class, stride=dil), :].astype(BF16)


def _layer0(x, mod, norm_g, w_in, conv_w, conv_b, ln_g, ln_b, w_out):
    nb, s, d = x.shape
    ts = L0_TILE
    dils = [dil for _, dil in DILATED_GROUPS if dil > 1]
    row = lambda a: a.reshape(1, d)
    const = lambda shape: pl.BlockSpec(shape, lambda b, t: (0,) * len(shape))
    return pl.pallas_call(
        _layer0_kernel,
        grid=(nb, s // ts),
        in_specs=[
            pl.BlockSpec((None, ts, d), lambda b, t: (b, t, 0)),
            pl.BlockSpec((None, None, 3, d), lambda b, t: (0, b, 0, 0)),
            pl.BlockSpec((None, None, 3, d), lambda b, t: (1, b, 0, 0)),
            const((1, d)), const((1, d)),
            const((d, 3 * d)),
            const((CONV_WIDTH, d)), const((1, d)), const((1, d)), const((1, d)),
            const((d, d)),
        ],
        out_specs=[
            pl.BlockSpec((None, ts, d), lambda b, t: (b, t, 0)),
            pl.BlockSpec((None, ts, d), lambda b, t: (b, t, 0)),
        ] + [pl.BlockSpec((None, dil, ts // dil, d), lambda b, t: (b, 0, t, 0)) for dil in dils],
        out_shape=[jax.ShapeDtypeStruct((nb, s, d), F32), jax.ShapeDtypeStruct((nb, s, d), BF16)]
        + [jax.ShapeDtypeStruct((nb, dil, s // dil, d), BF16) for dil in dils],
        scratch_shapes=[pltpu.VMEM((HALO + ts, d), F32),
                        pltpu.VMEM((SUBLANES - 1, HALO + ts - SUBLANES, d), F32),
                        pltpu.VMEM((ts, d), F32),
                        pltpu.VMEM((d // LANES, ts, LANES), F32)],
        compiler_params=_cparams(("arbitrary", "arbitrary")),
        name="layer0_conv_mixer",
    )(x, mod, mod, row(norm_g[0]), row(norm_g[1]), w_in.astype(BF16), conv_w, row(conv_b), row(ln_g),
      row(ln_b), w_out.astype(BF16))


def _proj_kernel(*refs, with_z):
    if with_z:
        h_ref, wq_ref, wkt_ref, wv_ref, gq_ref, gk_ref, wz_ref, q_ref, kt_ref, v_ref, z_ref = refs
    else:
        h_ref, wq_ref, wkt_ref, wv_ref, gq_ref, gk_ref, q_ref, kt_ref, v_ref = refs
    h = h_ref[...]
    tm = h.shape[0]
    lane = lax.broadcasted_iota(jnp.int32, (tm, LANES), 1)
    low = lane < HEAD_DIM

    q = jnp.dot(h, wq_ref[...], preferred_element_type=F32)
    for p in range(N_PAIRS):
        xq = q[:, p * LANES:(p + 1) * LANES]
        x2 = xq * xq
        s_lo = jnp.sum(jnp.where(low, x2, 0.0), axis=-1, keepdims=True)
        s_hi = jnp.sum(jnp.where(low, 0.0, x2), axis=-1, keepdims=True)
        inv = lax.rsqrt(jnp.where(low, s_lo, s_hi) * (1.0 / HEAD_DIM) + NORM_EPS)
        q_ref[p] = (xq * inv * gq_ref[...]).astype(BF16)

    kt = lax.dot_general(wkt_ref[...], h, (((1,), (1,)), ((), ())), preferred_element_type=F32)
    for tb in range(tm // BLOCK):
        blk = kt[:, tb * BLOCK:(tb + 1) * BLOCK].reshape(N_HEADS, HEAD_DIM, BLOCK)
        ss = jnp.sum(blk * blk, axis=1, keepdims=True)
        kn = blk * lax.rsqrt(ss * (1.0 / HEAD_DIM) + NORM_EPS) * gk_ref[...]
        kn = kn.reshape(N_PAIRS, 2 * HEAD_DIM, BLOCK).astype(BF16)
        for p in range(N_PAIRS):
            kt_ref[p, tb] = kn[p]

    v = jnp.dot(h, wv_ref[...], preferred_element_type=F32).astype(BF16)
    for p in range(N_PAIRS):
        v_ref[p] = v[:, p * LANES:(p + 1) * LANES]
    if with_z:
        z = jnp.dot(h, wz_ref[...], preferred_element_type=F32).astype(BF16)
        for p in range(N_PAIRS):
            z_ref[p] = z[:, p * LANES:(p + 1) * LANES]


def _project_group(h, g, wq, wkt, wv, gq, gk, wz):
    nb, s, d = h.shape
    tm = TOK_TILE
    steps = s // tm
    const = lambda shape: pl.BlockSpec(shape, lambda b, t: (0,) * len(shape))
    with_z = wz is not None
    in_specs = [pl.BlockSpec((None, tm, d), lambda b, t: (b, t, 0)), const((d, d)), const((d, d)),
                const((d, d)), const((1, LANES)), const((N_HEADS, HEAD_DIM, BLOCK))]
    args = [h, wq, wkt, wv, gq, gk]
    pm_spec = pl.BlockSpec((None, N_PAIRS, tm, LANES), lambda b, t: (b, 0, t, 0))
    pm_shape = jax.ShapeDtypeStruct((nb, N_PAIRS, s, LANES), BF16)
    out_specs = [pm_spec,
                 pl.BlockSpec((None, N_PAIRS, tm // BLOCK, LANES, BLOCK), lambda b, t: (b, 0, t, 0, 0)),
                 pm_spec]
    out_shape = [pm_shape, jax.ShapeDtypeStruct((nb, N_PAIRS, s // BLOCK, LANES, BLOCK), BF16), pm_shape]
    if with_z:
        in_specs.append(const((d, d)))
        args.append(wz)
        out_specs.append(pm_spec)
        out_shape.append(pm_shape)
    return pl.pallas_call(
        functools.partial(_proj_kernel, with_z=with_z),
        grid=(nb, steps),
        in_specs=in_specs,
        out_specs=out_specs,
        out_shape=out_shape,
        compiler_params=_cparams(("arbitrary", "arbitrary")),
        name=f"proj_group{g}",
    )(*args)


def _alibi_slopes():
    return np.exp2(-8.0 * np.arange(1, N_HEADS + 1, dtype=np.float64) / N_HEADS)


def _bias_tables(dilation):
    qi = np.arange(BLOCK)[:, None]
    kj = np.arange(2 * BLOCK)[None, :]
    steps = qi + BLOCK - kj
    n_steps = BLOCK
    valid = (steps >= 0) & (steps <= n_steps)
    valid_first = valid & (kj >= BLOCK)
    dist = (steps * dilation).astype(np.float64)
    slopes = _alibi_slopes()
    gen = np.empty((2, N_PAIRS, 2 * BLOCK, 2 * BLOCK), np.float32)
    for p in range(N_PAIRS):
        for half in range(2):
            sl = slopes[2 * p + half]
            rows = slice(half * BLOCK, (half + 1) * BLOCK)
            gen[0, p, rows] = np.where(valid, -sl * dist, NEG_INF)
            gen[1, p, rows] = np.where(valid_first, -sl * dist, NEG_INF)
    first = np.ascontiguousarray(gen[1, :, :, BLOCK:])
    return gen, first


def _attn_kernel(*refs, mode):
    if mode == "chain":
        q_ref, kt_ref, v_ref, ktp_ref, vp_ref, bias_ref, o_ref, st_ref = refs
    elif mode == "class4":
        q_ref, kt_ref, v_ref, bias_ref, bfirst_ref, o_ref, st_ref = refs
    else:
        q_ref, kt_ref, v_ref, bfirst_ref, o_ref, st_ref = refs

    lane = lax.broadcasted_iota(jnp.int32, (BLOCK, LANES), 1)
    low = lane < HEAD_DIM
    ones_cols = jnp.ones((BLOCK, LANES), BF16)
    is_start = (pl.program_id(1) == 0).astype(jnp.int32)

    for bi in range(Q_BLOCKS):
        rows = slice(bi * BLOCK, (bi + 1) * BLOCK)

        def pair_body(p, stats, bi=bi, rows=rows):
            qb = q_ref[p, rows, :]
            zero = jnp.zeros_like(qb)
            q2 = jnp.concatenate([jnp.where(low, qb, zero), jnp.where(low, zero, qb)], axis=0)
            k_cur = kt_ref[p, bi]
            v_cur = v_ref[p, rows, :]
            first = mode == "single" or (mode == "class4" and bi == 0)
            if first:
                kk = k_cur
                vv = jnp.concatenate([v_cur, ones_cols], axis=1)
                bias = bfirst_ref[p]
            else:
                if bi > 0:
                    k_prev = kt_ref[p, bi - 1]
                    v_prev = v_ref[p, (bi - 1) * BLOCK:bi * BLOCK, :]
                    bias = bias_ref[0, p]
                else:
                    k_prev = ktp_ref[p, 0]
                    v_prev = vp_ref[p]
                    bias = bias_ref[is_start, p]
                kk = jnp.concatenate([k_prev, k_cur], axis=1)
                vv = jnp.concatenate(
                    [jnp.concatenate([v_prev, v_cur], axis=0),
                     jnp.concatenate([ones_cols, ones_cols], axis=0)], axis=1)
            s = jnp.dot(q2, kk, preferred_element_type=F32) + bias
            m = jnp.max(s, axis=-1, keepdims=True)
            pe = jnp.exp(s - m).astype(BF16)
            r = jnp.dot(pe, vv, preferred_element_type=F32)
            o = jnp.where(low, r[0:BLOCK, 0:LANES], r[BLOCK:2 * BLOCK, 0:LANES]).astype(BF16)
            o_ref[p, rows, :] = o
            m_a = m[0:BLOCK]
            m_b = m[BLOCK:2 * BLOCK]
            l_a = r[0:BLOCK, LANES:2 * LANES]
            l_b = r[BLOCK:2 * BLOCK, LANES:2 * LANES]
            stats = jnp.where(lane == 2 * p, m_a, stats)
            stats = jnp.where(lane == 2 * p + 1, m_b, stats)
            stats = jnp.where(lane == N_HEADS + 2 * p, l_a, stats)
            stats = jnp.where(lane == N_HEADS + 2 * p + 1, l_b, stats)
            return stats

        stats = jnp.ones((BLOCK, LANES), F32)
        for p in range(N_PAIRS):
            stats = pair_body(p, stats)
        st_ref[rows, :] = stats


def _attention_group(g, dilation, q, kt, v):
    nb, _, s, _ = q.shape
    length = s // dilation
    tm = TOK_TILE
    steps = s // tm
    gen, first = _bias_tables(dilation)
    const = lambda shape: pl.BlockSpec(shape, lambda b, t: (0,) * len(shape))
    q_spec = pl.BlockSpec((None, N_PAIRS, tm, LANES), lambda b, t: (b, 0, t, 0))
    kt_spec = pl.BlockSpec((None, N_PAIRS, Q_BLOCKS, LANES, BLOCK), lambda b, t: (b, 0, t, 0, 0))
    in_specs = [q_spec, kt_spec, q_spec]
    args = [q, kt, v]
    if length > tm:
        mode = "chain"
        prev = lambda t: jnp.maximum(t * Q_BLOCKS - 1, 0)
        in_specs += [
            pl.BlockSpec((None, N_PAIRS, 1, LANES, BLOCK), lambda b, t: (b, 0, prev(t), 0, 0)),
            pl.BlockSpec((None, N_PAIRS, BLOCK, LANES), lambda b, t: (b, 0, prev(t), 0)),
            const(gen.shape),
        ]
        args += [kt, v, jnp.asarray(gen)]
    elif length == tm:
        mode = "class4"
        in_specs += [const(gen[0:1].shape), const(first.shape)]
        args += [jnp.asarray(gen[0:1]), jnp.asarray(first)]
    else:
        assert length == BLOCK
        mode = "single"
        in_specs += [const(first.shape)]
        args += [jnp.asarray(first)]

    return pl.pallas_call(
        functools.partial(_attn_kernel, mode=mode),
        grid=(nb, steps),
        in_specs=in_specs,
        out_specs=[q_spec, pl.BlockSpec((None, tm, LANES), lambda b, t: (b, t, 0))],
        out_shape=[jax.ShapeDtypeStruct((nb, N_PAIRS, s, LANES), BF16), jax.ShapeDtypeStruct((nb, s, LANES), F32)],
        compiler_params=_cparams(("arbitrary", "arbitrary")),
        name=f"attn_group{g}",
    )(*args)


def _merge_kernel(o1_ref, o2_ref, o3_ref, s1_ref, s2_ref, s3_ref, z_ref, x_ref, mod_ref, e_ref, wout_ref,
                  out_ref, nat2, nat3, stn2, stn3):
    o_refs = [o1_ref, o2_ref, o3_ref]
    st_refs = [s1_ref, s2_ref, s3_ref]
    nat_bufs = [None, nat2, nat3]
    stn_bufs = [None, stn2, stn3]
    stats, ogs = [], []
    for g in range(3):
        dil, per_class = st_refs[g].shape[0], st_refs[g].shape[1]
        if dil == 1:
            stats.append(st_refs[g][0])
            ogs.append(jnp.concatenate([o_refs[g][p, 0] for p in range(N_PAIRS)], axis=1).astype(F32))
            continue
        for r in range(dil):
            dst = pl.ds(r, per_class, stride=dil)
            stn_bufs[g][dst, :] = st_refs[g][r]
            for p in range(N_PAIRS):
                nat_bufs[g][p, dst, :] = o_refs[g][p, r].astype(F32)
        stats.append(stn_bufs[g][...])
        ogs.append(jnp.concatenate([nat_bufs[g][p] for p in range(N_PAIRS)], axis=1))
    dens = [pltpu.roll(st, LANES - N_HEADS, axis=1) for st in stats]
    m = jnp.maximum(jnp.maximum(stats[0], stats[1]), stats[2])
    es = [jnp.exp(st - m) for st in stats]
    den = es[0] * dens[0] + es[1] * dens[1] + es[2] * dens[2]
    inv = 1.0 / den
    lane = lax.broadcasted_iota(jnp.int32, m.shape, 1)
    acc = None
    for g in range(3):
        w = jnp.where(lane < N_HEADS, es[g] * inv, 0.0)
        w_hi = w.astype(BF16)
        w_lo = (w - w_hi.astype(F32)).astype(BF16)
        wx = jnp.dot(jnp.concatenate([w_hi, w_lo], axis=1), e_ref[...], preferred_element_type=F32)
        acc = wx * ogs[g] if acc is None else acc + wx * ogs[g]
    z = jnp.concatenate([z_ref[p] for p in range(N_PAIRS)], axis=1).astype(F32)
    a = (acc * _silu(z)).astype(BF16)
    y = jnp.dot(a, wout_ref[...], preferred_element_type=F32)
    out_ref[...] = x_ref[...] + mod_ref[2:3, :] * y


def _merge(os_, stats, z, x1, mod, w_out):
    nb, s, d = x1.shape
    tm = TOK_TILE
    expand = np.zeros((2 * LANES, d), np.float32)
    for hd in range(N_HEADS):
        expand[hd, hd * HEAD_DIM:(hd + 1) * HEAD_DIM] = 1.0
        expand[LANES + hd, hd * HEAD_DIM:(hd + 1) * HEAD_DIM] = 1.0
    pm_spec = pl.BlockSpec((None, N_PAIRS, tm, LANES), lambda b, t: (b, 0, t, 0))
    x_spec = pl.BlockSpec((None, tm, d), lambda b, t: (b, t, 0))
    const = lambda shape: pl.BlockSpec(shape, lambda b, t: (0,) * len(shape))
    o_views, o_specs, st_views, st_specs = [], [], [], []
    for (_, dil), o, st in zip(DILATED_GROUPS, os_, stats):
        o_views.append(o.reshape(nb, N_PAIRS, dil, s // dil, LANES))
        o_specs.append(pl.BlockSpec((None, N_PAIRS, dil, tm // dil, LANES), lambda b, t: (b, 0, 0, t, 0)))
        st_views.append(st.reshape(nb, dil, s // dil, LANES))
        st_specs.append(pl.BlockSpec((None, dil, tm // dil, LANES), lambda b, t: (b, 0, t, 0)))
    return pl.pallas_call(
        _merge_kernel,
        grid=(nb, s // tm),
        in_specs=o_specs + st_specs + [
            pm_spec, x_spec, pl.BlockSpec((None, None, 3, d), lambda b, t: (1, b, 0, 0)),
            const((2 * LANES, d)), const((d, d))],
        out_specs=x_spec,
        out_shape=jax.ShapeDtypeStruct((nb, s, d), F32),
        scratch_shapes=[pltpu.VMEM((N_PAIRS, tm, LANES), F32), pltpu.VMEM((N_PAIRS, tm, LANES), F32),
                        pltpu.VMEM((tm, LANES), F32), pltpu.VMEM((tm, LANES), F32)],
        compiler_params=_cparams(("arbitrary", "arbitrary")),
        name="merge_out_proj",
    )(*o_views, *st_views, z, x1, mod, jnp.asarray(expand, BF16), w_out.astype(BF16))


def kernel(x, c, norm_g, ada_w, ada_b, a_w_in, a_conv_w, a_conv_b, a_ln_g, a_ln_b, a_w_out, b_w_in, b_q_norm,
           b_k_norm, b_w_out):
    d = D_MODEL
    mod = _ada_modulation(c, ada_w, ada_b)
    x1, *h1s = _layer0(x, mod, norm_g, a_w_in[0], a_conv_w[0], a_conv_b[0], a_ln_g[0], a_ln_b[0], a_w_out[0])
    h1s = [h.reshape(x.shape) for h in h1s]

    w_b = b_w_in[0]
    wz = w_b[:, 9 * d:10 * d].astype(BF16)
    scale = HEAD_DIM ** -0.5
    os_, stats, z = [], [], None
    for g, (_, dilation) in enumerate(DILATED_GROUPS):
        base = 3 * g * d
        wq = w_b[:, base:base + d].astype(BF16)
        wkt = w_b[:, base + d:base + 2 * d].T.astype(BF16)
        wv = w_b[:, base + 2 * d:base + 3 * d].astype(BF16)
        gq = jnp.tile(b_q_norm[0, g] * scale, 2).reshape(1, LANES)
        gk = jnp.broadcast_to(b_k_norm[0, g][None, :, None], (N_HEADS, HEAD_DIM, BLOCK))
        outs = _project_group(h1s[g], g, wq, wkt, wv, gq, gk, wz if g == 0 else None)
        if g == 0:
            q, kt, v, z = outs
        else:
            q, kt, v = outs
        o, st = _attention_group(g, dilation, q, kt, v)
        os_.append(o)
        stats.append(st)
    return _merge(os_, stats, z, x1, mod, b_w_out[0])
```

```python
import functools

import jax
import jax.numpy as jnp
import numpy as np
from jax import lax
from jax.experimental import pallas as pl
from jax.experimental.pallas import tpu as pltpu

F32 = jnp.float32
BF16 = jnp.bfloat16

D_MODEL = 1024
CONV_WIDTH = 31
HEAD_DIM = 64
N_HEADS = 16
N_PAIRS = N_HEADS // 2
DILATED_GROUPS = ((128, 1), (512, 4), (2048, 16))
BLOCK = 128
NORM_EPS = 1e-6
NEG_INF = -1e30
LANES = 128
SUBLANES = 8
VMEM_LIMIT = 56 * 1024 * 1024

TOK_TILE = 512
Q_BLOCKS = TOK_TILE // BLOCK
L0_TILE = 256
HALO = 32
CONV_ROWS = 128


def _sigmoid(x):
    return 1.0 / (1.0 + jnp.exp(-x))


def _silu(x):
    return x * _sigmoid(x)


def _cparams(sem):
    return pltpu.CompilerParams(dimension_semantics=sem, vmem_limit_bytes=VMEM_LIMIT)


def _ada_kernel(c_ref, w_ref, b_ref, o_ref):
    c = c_ref[...]
    sc = _silu(c).astype(BF16)
    o_ref[...] = jnp.dot(sc, w_ref[...].astype(BF16), preferred_element_type=F32) + b_ref[...]


def _ada_modulation(c, ada_w, ada_b):
    depth, d, d3 = ada_w.shape
    nb = c.shape[0]
    cols = 768
    out = pl.pallas_call(
        _ada_kernel,
        grid=(depth, d3 // cols),
        in_specs=[
            pl.BlockSpec((nb, d), lambda l, j: (0, 0)),
            pl.BlockSpec((None, d, cols), lambda l, j: (l, 0, j)),
            pl.BlockSpec((None, 1, cols), lambda l, j: (l, 0, j)),
        ],
        out_specs=pl.BlockSpec((None, nb, cols), lambda l, j: (l, 0, j)),
        out_shape=jax.ShapeDtypeStruct((depth, nb, d3), F32),
        compiler_params=_cparams(("arbitrary", "arbitrary")),
        name="ada_modulation",
    )(c, ada_w, ada_b.reshape(depth, 1, d3))
    return out.reshape(depth, nb, 3, d)


def _layer0_kernel(x_ref, mod0_ref, mod1_ref, g0_ref, g1_ref, win_ref, cw_ref, cb_ref, lg_ref, lb_ref,
                   wout_ref, x1_ref, h1_ref, h1c4_ref, h1c16_ref, ubuf, sbuf, cbuf, hstage):
    ts = x_ref.shape[0]
    d = x_ref.shape[1]

    @pl.when(pl.program_id(1) == 0)
    def _():
        ubuf[0:HALO, :] = jnp.zeros((HALO, d), F32)

    x = x_ref[...]
    shift, scale, gate = mod0_ref[0:1, :], mod0_ref[1:2, :], mod0_ref[2:3, :]
    ms = jnp.mean(x * x, axis=-1, keepdims=True)
    h = x * lax.rsqrt(ms + NORM_EPS) * (g0_ref[...] * (1.0 + scale)) + shift
    proj = jnp.dot(h.astype(BF16), win_ref[...], preferred_element_type=F32)
    u = proj[:, 0:d] * _sigmoid(proj[:, d:2 * d])
    sz = _silu(proj[:, 2 * d:3 * d])
    ubuf[HALO:HALO + ts, :] = u

    off = HALO - (CONV_WIDTH - 1)
    shifted_rows = sbuf.shape[1]
    for s in range(1, SUBLANES):
        sbuf[s - 1] = ubuf[s:s + shifted_rows, :]
    for ls in range(d // LANES):
        lsl = slice(ls * LANES, (ls + 1) * LANES)
        for rc in range(ts // CONV_ROWS):
            r0 = rc * CONV_ROWS
            acc = jnp.broadcast_to(cb_ref[0:1, lsl], (CONV_ROWS, LANES))
            for j in range(CONV_WIDTH):
                s = (off + j) % SUBLANES
                a = r0 + off + j - s
                if s == 0:
                    tap = ubuf[a:a + CONV_ROWS, lsl]
                else:
                    tap = sbuf[s - 1, a:a + CONV_ROWS, lsl]
                acc = acc + cw_ref[j:j + 1, lsl] * tap
            cbuf[r0:r0 + CONV_ROWS, lsl] = acc
    ubuf[0:HALO, :] = ubuf[ts:ts + HALO, :]

    cv = cbuf[...]
    mu = jnp.mean(cv, axis=-1, keepdims=True)
    xc = cv - mu
    var = jnp.mean(xc * xc, axis=-1, keepdims=True)
    y = xc * lax.rsqrt(var + NORM_EPS) * lg_ref[...] + lb_ref[...]
    a = (_silu(y) * sz).astype(BF16)
    out = jnp.dot(a, wout_ref[...], preferred_element_type=F32)
    x1 = x + gate * out
    x1_ref[...] = x1

    shift1, scale1 = mod1_ref[0:1, :], mod1_ref[1:2, :]
    ms1 = jnp.mean(x1 * x1, axis=-1, keepdims=True)
    h1 = x1 * lax.rsqrt(ms1 + NORM_EPS) * (g1_ref[...] * (1.0 + scale1)) + shift1
    h1_ref[...] = h1.astype(BF16)
    for ls in range(d // LANES):
        lsl = slice(ls * LANES, (ls + 1) * LANES)
        hstage[ls] = h1[:, lsl]
        for ref in (h1c4_ref, h1c16_ref):
            dil, per_class = ref.shape[0], ref.shape[1]
            for r in range(dil):
                ref[r, :, lsl] = hstage[ls, pl.ds(r, per_class, stride=dil), :].astype(BF16)


def _layer0(x, mod, norm_g, w_in, conv_w, conv_b, ln_g, ln_b, w_out):
    nb, s, d = x.shape
    ts = L0_TILE
    dils = [dil for _, dil in DILATED_GROUPS if dil > 1]
    row = lambda a: a.reshape(1, d)
    const = lambda shape: pl.BlockSpec(shape, lambda b, t: (0,) * len(shape))
    return pl.pallas_call(
        _layer0_kernel,
        grid=(nb, s // ts),
        in_specs=[
            pl.BlockSpec((None, ts, d), lambda b, t: (b, t, 0)),
            pl.BlockSpec((None, None, 3, d), lambda b, t: (0, b, 0, 0)),
            pl.BlockSpec((None, None, 3, d), lambda b, t: (1, b, 0, 0)),
            const((1, d)), const((1, d)),
            const((d, 3 * d)),
            const((CONV_WIDTH, d)), const((1, d)), const((1, d)), const((1, d)),
            const((d, d)),
        ],
        out_specs=[
            pl.BlockSpec((None, ts, d), lambda b, t: (b, t, 0)),
            pl.BlockSpec((None, ts, d), lambda b, t: (b, t, 0)),
        ] + [pl.BlockSpec((None, dil, ts // dil, d), lambda b, t: (b, 0, t, 0)) for dil in dils],
        out_shape=[jax.ShapeDtypeStruct((nb, s, d), F32), jax.ShapeDtypeStruct((nb, s, d), BF16)]
        + [jax.ShapeDtypeStruct((nb, dil, s // dil, d), BF16) for dil in dils],
        scratch_shapes=[pltpu.VMEM((HALO + ts, d), F32),
                        pltpu.VMEM((SUBLANES - 1, HALO + ts - SUBLANES, d), F32),
                        pltpu.VMEM((ts, d), F32),
                        pltpu.VMEM((d // LANES, ts, LANES), F32)],
        compiler_params=_cparams(("arbitrary", "arbitrary")),
        name="layer0_conv_mixer",
    )(x, mod, mod, row(norm_g[0]), row(norm_g[1]), w_in.astype(BF16), conv_w, row(conv_b), row(ln_g),
      row(ln_b), w_out.astype(BF16))


def _store_rows(ref, x):
    for p in range(N_PAIRS):
        ref[p] = x[:, p * LANES:(p + 1) * LANES]


def _store_cols(ref, xt):
    for tb in range(xt.shape[1] // BLOCK):
        for p in range(N_PAIRS):
            ref[p, tb] = xt[p * LANES:(p + 1) * LANES, tb * BLOCK:(tb + 1) * BLOCK]


def _proj_kernel(*refs, plain_transposed, with_z):
    if with_z:
        h_ref, wn_ref, wt_ref, wp_ref, gn_ref, gt_ref, wz_ref, n_ref, t_ref, p_ref, z_ref = refs
    else:
        h_ref, wn_ref, wt_ref, wp_ref, gn_ref, gt_ref, n_ref, t_ref, p_ref = refs
    h = h_ref[...]
    tm = h.shape[0]
    nt_dims = (((1,), (1,)), ((), ()))
    lane = lax.broadcasted_iota(jnp.int32, (tm, LANES), 1)
    low = lane < HEAD_DIM

    xn = jnp.dot(h, wn_ref[...], preferred_element_type=F32)
    for p in range(N_PAIRS):
        xq = xn[:, p * LANES:(p + 1) * LANES]
        x2 = xq * xq
        s_lo = jnp.sum(jnp.where(low, x2, 0.0), axis=-1, keepdims=True)
        s_hi = jnp.sum(jnp.where(low, 0.0, x2), axis=-1, keepdims=True)
        inv = lax.rsqrt(jnp.where(low, s_lo, s_hi) * (1.0 / HEAD_DIM) + NORM_EPS)
        n_ref[p] = (xq * inv * gn_ref[...]).astype(BF16)

    xt = lax.dot_general(wt_ref[...], h, nt_dims, preferred_element_type=F32)
    for tb in range(tm // BLOCK):
        blk = xt[:, tb * BLOCK:(tb + 1) * BLOCK].reshape(N_HEADS, HEAD_DIM, BLOCK)
        ss = jnp.sum(blk * blk, axis=1, keepdims=True)
        tn = blk * lax.rsqrt(ss * (1.0 / HEAD_DIM) + NORM_EPS) * gt_ref[...]
        tn = tn.reshape(N_PAIRS, 2 * HEAD_DIM, BLOCK).astype(BF16)
        for p in range(N_PAIRS):
            t_ref[p, tb] = tn[p]

    if plain_transposed:
        _store_cols(p_ref, lax.dot_general(wp_ref[...], h, nt_dims, preferred_element_type=F32).astype(BF16))
    else:
        _store_rows(p_ref, jnp.dot(h, wp_ref[...], preferred_element_type=F32).astype(BF16))
    if with_z:
        _store_rows(z_ref, jnp.dot(h, wz_ref[...], preferred_element_type=F32).astype(BF16))


def _project_group(h, g, wn, wt, wp, gn, gt, wz, plain_transposed):
    nb, s, d = h.shape
    tm = TOK_TILE
    steps = s // tm
    const = lambda shape: pl.BlockSpec(shape, lambda b, t: (0,) * len(shape))
    with_z = wz is not None
    in_specs = [pl.BlockSpec((None, tm, d), lambda b, t: (b, t, 0)), const((d, d)), const((d, d)),
                const((d, d)), const((1, LANES)), const((N_HEADS, HEAD_DIM, BLOCK))]
    args = [h, wn, wt, wp, gn, gt]
    pm_spec = pl.BlockSpec((None, N_PAIRS, tm, LANES), lambda b, t: (b, 0, t, 0))
    pm_shape = jax.ShapeDtypeStruct((nb, N_PAIRS, s, LANES), BF16)
    fm_spec = pl.BlockSpec((None, N_PAIRS, tm // BLOCK, LANES, BLOCK), lambda b, t: (b, 0, t, 0, 0))
    fm_shape = jax.ShapeDtypeStruct((nb, N_PAIRS, s // BLOCK, LANES, BLOCK), BF16)
    out_specs = [pm_spec, fm_spec, fm_spec if plain_transposed else pm_spec]
    out_shape = [pm_shape, fm_shape, fm_shape if plain_transposed else pm_shape]
    if with_z:
        in_specs.append(const((d, d)))
        args.append(wz)
        out_specs.append(pm_spec)
        out_shape.append(pm_shape)
    return pl.pallas_call(
        functools.partial(_proj_kernel, plain_transposed=plain_transposed, with_z=with_z),
        grid=(nb, steps),
        in_specs=in_specs,
        out_specs=out_specs,
        out_shape=out_shape,
        compiler_params=_cparams(("arbitrary", "arbitrary")),
        name=f"proj_group{g}",
    )(*args)


def _alibi_slopes():
    return np.exp2(-8.0 * np.arange(1, N_HEADS + 1, dtype=np.float64) / N_HEADS)


def _bias_tables(dilation):
    kj = np.arange(2 * BLOCK)[:, None]
    qi = np.arange(BLOCK)[None, :]
    steps = qi + BLOCK - kj
    n_steps = BLOCK
    valid = (steps >= 0) & (steps <= n_steps)
    valid_first = valid & (kj >= BLOCK)
    dist = (steps * dilation).astype(np.float64)
    slopes = _alibi_slopes()
    gen = np.empty((2, N_PAIRS, 2 * BLOCK, 2 * BLOCK), np.float32)
    for p in range(N_PAIRS):
        for half in range(2):
            sl = slopes[2 * p + half]
            cols = slice(half * BLOCK, (half + 1) * BLOCK)
            gen[0, p, :, cols] = np.where(valid, -sl * dist, NEG_INF)
            gen[1, p, :, cols] = np.where(valid_first, -sl * dist, NEG_INF)
    first = np.ascontiguousarray(gen[1, :, BLOCK:, :])
    return gen, first


def _attn_banded_kernel(*refs, mode):
    if mode == "chain":
        q_ref, kt_ref, v_ref, ktp_ref, vp_ref, bias_ref, o_ref, st_ref = refs
    else:
        q_ref, kt_ref, v_ref, bias_ref, o_ref, st_ref = refs
    lane = lax.broadcasted_iota(jnp.int32, (BLOCK, LANES), 1)
    low = lane < HEAD_DIM
    ones_cols = jnp.ones((2 * BLOCK, LANES), BF16)
    is_start = (pl.program_id(1) == 0).astype(jnp.int32)
    for bi in range(Q_BLOCKS):
        rows = slice(bi * BLOCK, (bi + 1) * BLOCK)
        stats = jnp.ones((BLOCK, LANES), F32)
        for p in range(N_PAIRS):
            qb = q_ref[p, rows, :]
            zero = jnp.zeros_like(qb)
            q2 = jnp.concatenate([jnp.where(low, qb, zero), jnp.where(low, zero, qb)], axis=0)
            k_cur = kt_ref[p, bi]
            v_cur = v_ref[p, rows, :]
            if bi > 0:
                k_prev, v_prev, bias = kt_ref[p, bi - 1], v_ref[p, (bi - 1) * BLOCK:bi * BLOCK, :], bias_ref[0, p]
            elif mode == "chain":
                k_prev, v_prev, bias = ktp_ref[p, 0], vp_ref[p], bias_ref[is_start, p]
            else:
                k_prev, v_prev, bias = k_cur, v_cur, bias_ref[1, p]
            kk = jnp.concatenate([k_prev, k_cur], axis=1)
            vv = jnp.concatenate([jnp.concatenate([v_prev, v_cur], axis=0), ones_cols], axis=1)
            s = jnp.dot(q2, kk, preferred_element_type=F32) + bias
            m = jnp.max(s, axis=-1, keepdims=True)
            pe = jnp.exp(s - m).astype(BF16)
            r = jnp.dot(pe, vv, preferred_element_type=F32)
            o_ref[p, rows, :] = jnp.where(low, r[0:BLOCK, 0:LANES], r[BLOCK:2 * BLOCK, 0:LANES]).astype(BF16)
            stats = jnp.where(lane == 2 * p, m[0:BLOCK], stats)
            stats = jnp.where(lane == 2 * p + 1, m[BLOCK:2 * BLOCK], stats)
            stats = jnp.where(lane == N_HEADS + 2 * p, r[0:BLOCK, LANES:2 * LANES], stats)
            stats = jnp.where(lane == N_HEADS + 2 * p + 1, r[BLOCK:2 * BLOCK, LANES:2 * LANES], stats)
        st_ref[rows, :] = stats


def _attn_single_kernel(k_ref, qt_ref, vt_ref, bias_ref, o_ref, st_ref, stt):
    zeros = jnp.zeros((HEAD_DIM, BLOCK), BF16)
    stt[...] = jnp.ones(stt.shape, F32)
    for bi in range(Q_BLOCKS):
        rows = slice(bi * BLOCK, (bi + 1) * BLOCK)
        for p in range(N_PAIRS):
            qt = qt_ref[p, bi]
            q2t = jnp.concatenate(
                [jnp.concatenate([qt[0:HEAD_DIM], zeros], axis=0),
                 jnp.concatenate([zeros, qt[HEAD_DIM:2 * HEAD_DIM]], axis=0)], axis=1)
            st = jnp.dot(k_ref[p, rows, :], q2t, preferred_element_type=F32) + bias_ref[p]
            m = jnp.max(st, axis=0, keepdims=True)
            pe = jnp.exp(st - m)
            den = jnp.sum(pe, axis=0, keepdims=True)
            ot = jnp.dot(vt_ref[p, bi], pe.astype(BF16), preferred_element_type=F32)
            o_t = jnp.concatenate([ot[0:HEAD_DIM, 0:BLOCK], ot[HEAD_DIM:2 * HEAD_DIM, BLOCK:2 * BLOCK]], axis=0)
            o_ref[p, rows, :] = o_t.T.astype(BF16)
            stt[2 * p:2 * p + 1, :] = m[:, 0:BLOCK]
            stt[2 * p + 1:2 * p + 2, :] = m[:, BLOCK:2 * BLOCK]
            stt[N_HEADS + 2 * p:N_HEADS + 2 * p + 1, :] = den[:, 0:BLOCK]
            stt[N_HEADS + 2 * p + 1:N_HEADS + 2 * p + 2, :] = den[:, BLOCK:2 * BLOCK]
        st_ref[rows, :] = stt[...].T


def _attention_group(g, dilation, tok_major, feat_major, plain):
    nb, _, s, _ = tok_major.shape
    length = s // dilation
    tm = TOK_TILE
    gen_t, first_t = _bias_tables(dilation)
    const = lambda shape: pl.BlockSpec(shape, lambda b, t: (0,) * len(shape))
    pm_spec = pl.BlockSpec((None, N_PAIRS, tm, LANES), lambda b, t: (b, 0, t, 0))
    fm_spec = pl.BlockSpec((None, N_PAIRS, Q_BLOCKS, LANES, BLOCK), lambda b, t: (b, 0, t, 0, 0))
    scratch = []
    if length == BLOCK:
        body = _attn_single_kernel
        in_specs = [pm_spec, fm_spec, fm_spec, const(first_t.shape)]
        args = [tok_major, feat_major, plain, jnp.asarray(first_t)]
        scratch = [pltpu.VMEM((BLOCK, LANES), F32)]
    else:
        gen = np.ascontiguousarray(np.concatenate(
            [gen_t[..., 0:BLOCK].swapaxes(2, 3), gen_t[..., BLOCK:].swapaxes(2, 3)], axis=2))
        in_specs = [pm_spec, fm_spec, pm_spec]
        args = [tok_major, feat_major, plain]
        if length > tm:
            mode = "chain"
            prev = lambda t: jnp.maximum(t * Q_BLOCKS - 1, 0)
            in_specs += [
                pl.BlockSpec((None, N_PAIRS, 1, LANES, BLOCK), lambda b, t: (b, 0, prev(t), 0, 0)),
                pl.BlockSpec((None, N_PAIRS, BLOCK, LANES), lambda b, t: (b, 0, prev(t), 0)),
            ]
            args += [feat_major, plain]
        else:
            assert length == tm
            mode = "class"
        in_specs.append(const(gen.shape))
        args.append(jnp.asarray(gen))
        body = functools.partial(_attn_banded_kernel, mode=mode)
    return pl.pallas_call(
        body,
        grid=(nb, s // tm),
        in_specs=in_specs,
        out_specs=[pm_spec, pl.BlockSpec((None, tm, LANES), lambda b, t: (b, t, 0))],
        out_shape=[jax.ShapeDtypeStruct((nb, N_PAIRS, s, LANES), BF16), jax.ShapeDtypeStruct((nb, s, LANES), F32)],
        scratch_shapes=scratch,
        compiler_params=_cparams(("arbitrary", "arbitrary")),
        name=f"attn_group{g}",
    )(*args)


def _merge_kernel(o1_ref, o2_ref, o3_ref, s1_ref, s2_ref, s3_ref, z_ref, x_ref, mod_ref, e_ref, wout_ref,
                  out_ref, nat2, nat3, stn2, stn3):
    o_refs = [o1_ref, o2_ref, o3_ref]
    st_refs = [s1_ref, s2_ref, s3_ref]
    nat_bufs = [None, nat2, nat3]
    stn_bufs = [None, stn2, stn3]
    stats, ogs = [], []
    for g in range(3):
        dil, per_class = st_refs[g].shape[0], st_refs[g].shape[1]
        if dil == 1:
            stats.append(st_refs[g][0])
            ogs.append(jnp.concatenate([o_refs[g][p, 0] for p in range(N_PAIRS)], axis=1).astype(F32))
            continue
        for r in range(dil):
            dst = pl.ds(r, per_class, stride=dil)
            stn_bufs[g][dst, :] = st_refs[g][r]
            for p in range(N_PAIRS):
                nat_bufs[g][p, dst, :] = o_refs[g][p, r].astype(F32)
        stats.append(stn_bufs[g][...])
        ogs.append(jnp.concatenate([nat_bufs[g][p] for p in range(N_PAIRS)], axis=1))
    dens = [pltpu.roll(st, LANES - N_HEADS, axis=1) for st in stats]
    m = jnp.maximum(jnp.maximum(stats[0], stats[1]), stats[2])
    es = [jnp.exp(st - m) for st in stats]
    den = es[0] * dens[0] + es[1] * dens[1] + es[2] * dens[2]
    inv = 1.0 / den
    lane = lax.broadcasted_iota(jnp.int32, m.shape, 1)
    acc = None
    for g in range(3):
        w = jnp.where(lane < N_HEADS, es[g] * inv, 0.0)
        w_hi = w.astype(BF16)
        w_lo = (w - w_hi.astype(F32)).astype(BF16)
        wx = jnp.dot(jnp.concatenate([w_hi, w_lo], axis=1), e_ref[...], preferred_element_type=F32)
        acc = wx * ogs[g] if acc is None else acc + wx * ogs[g]
    z = jnp.concatenate([z_ref[p] for p in range(N_PAIRS)], axis=1).astype(F32)
    a = (acc * _silu(z)).astype(BF16)
    y = jnp.dot(a, wout_ref[...], preferred_element_type=F32)
    out_ref[...] = x_ref[...] + mod_ref[2:3, :] * y


def _merge(os_, stats, z, x1, mod, w_out):
    nb, s, d = x1.shape
    tm = TOK_TILE
    expand = np.zeros((2 * LANES, d), np.float32)
    for hd in range(N_HEADS):
        expand[hd, hd * HEAD_DIM:(hd + 1) * HEAD_DIM] = 1.0
        expand[LANES + hd, hd * HEAD_DIM:(hd + 1) * HEAD_DIM] = 1.0
    pm_spec = pl.BlockSpec((None, N_PAIRS, tm, LANES), lambda b, t: (b, 0, t, 0))
    x_spec = pl.BlockSpec((None, tm, d), lambda b, t: (b, t, 0))
    const = lambda shape: pl.BlockSpec(shape, lambda b, t: (0,) * len(shape))
    o_views, o_specs, st_views, st_specs = [], [], [], []
    for (_, dil), o, st in zip(DILATED_GROUPS, os_, stats):
        o_views.append(o.reshape(nb, N_PAIRS, dil, s // dil, LANES))
        o_specs.append(pl.BlockSpec((None, N_PAIRS, dil, tm // dil, LANES), lambda b, t: (b, 0, 0, t, 0)))
        st_views.append(st.reshape(nb, dil, s // dil, LANES))
        st_specs.append(pl.BlockSpec((None, dil, tm // dil, LANES), lambda b, t: (b, 0, t, 0)))
    return pl.pallas_call(
        _merge_kernel,
        grid=(nb, s // tm),
        in_specs=o_specs + st_specs + [
            pm_spec, x_spec, pl.BlockSpec((None, None, 3, d), lambda b, t: (1, b, 0, 0)),
            const((2 * LANES, d)), const((d, d))],
        out_specs=x_spec,
        out_shape=jax.ShapeDtypeStruct((nb, s, d), F32),
        scratch_shapes=[pltpu.VMEM((N_PAIRS, tm, LANES), F32), pltpu.VMEM((N_PAIRS, tm, LANES), F32),
                        pltpu.VMEM((tm, LANES), F32), pltpu.VMEM((tm, LANES), F32)],
        compiler_params=_cparams(("arbitrary", "arbitrary")),
        name="merge_out_proj",
    )(*o_views, *st_views, z, x1, mod, jnp.asarray(expand, BF16), w_out.astype(BF16))


def kernel(x, c, norm_g, ada_w, ada_b, a_w_in, a_conv_w, a_conv_b, a_ln_g, a_ln_b, a_w_out, b_w_in, b_q_norm,
           b_k_norm, b_w_out):
    d = D_MODEL
    s = x.shape[1]
    mod =_ada_modulation(c, ada_w, ada_b)
    x1, *h1s = _layer0(x, mod, norm_g, a_w_in[0], a_conv_w[0], a_conv_b[0], a_ln_g[0], a_ln_b[0], a_w_out[0])
    h1s = [h.reshape(x.shape) for h in h1s]

    w_b = b_w_in[0]
    wz = w_b[:, 9 * d:10 * d].astype(BF16)
    scale = HEAD_DIM ** -0.5
    os_, stats, z = [], [], None
    for g, (_, dilation) in enumerate(DILATED_GROUPS):
        base = 3 * g * d
        wq, wk, wv = (w_b[:, base + i * d:base + (i + 1) * d] for i in range(3))
        gq, gk = b_q_norm[0, g] * scale, b_k_norm[0, g]
        row_gain = lambda gain: jnp.tile(gain, 2).reshape(1, LANES)
        col_gain = lambda gain: jnp.broadcast_to(gain[None, :, None], (N_HEADS, HEAD_DIM, BLOCK))
        if s // dilation == BLOCK:
            outs = _project_group(h1s[g], g, wk.astype(BF16), wq.T.astype(BF16), wv.T.astype(BF16),
                                  row_gain(gk), col_gain(gq), None, True)
        else:
            outs = _project_group(h1s[g], g, wq.astype(BF16), wk.T.astype(BF16), wv.astype(BF16),
                                  row_gain(gq), col_gain(gk), wz if g == 0 else None, False)
        if g == 0:
            *outs, z = outs
        o, st = _attention_group(g, dilation, *outs)
        os_.append(o)
        stats.append(st)
    return _merge(os_, stats, z, x1, mod, b_w_out[0])
```

```python
import functools

import jax
import jax.numpy as jnp
import numpy as np
from jax import lax
from jax.experimental import pallas as pl
from jax.experimental.pallas import tpu as pltpu

F32 = jnp.float32
BF16 = jnp.bfloat16

D_MODEL = 1024
CONV_WIDTH = 31
HEAD_DIM = 64
N_HEADS = 16
N_PAIRS = N_HEADS // 2
DILATED_GROUPS = ((128, 1), (512, 4), (2048, 16))
BLOCK = 128
NORM_EPS = 1e-6
NEG_INF = -1e30
LANES = 128
SUBLANES = 8
VMEM_LIMIT = 56 * 1024 * 1024

TOK_TILE = 512
Q_BLOCKS = TOK_TILE // BLOCK
L0_TILE = 256
HALO = 32
CONV_ROWS = 128


def _sigmoid(x):
    return jax.nn.sigmoid(x)


def _silu(x):
    return x * _sigmoid(x)


def _cparams(sem):
    return pltpu.CompilerParams(dimension_semantics=sem, vmem_limit_bytes=VMEM_LIMIT)


def _ada_kernel(c_ref, w_ref, b_ref, o_ref):
    c = c_ref[...]
    sc = _silu(c).astype(BF16)
    o_ref[...] = jnp.dot(sc, w_ref[...].astype(BF16), preferred_element_type=F32) + b_ref[...]


def _ada_modulation(c, ada_w, ada_b):
    depth, d, d3 = ada_w.shape
    nb = c.shape[0]
    cols = 768
    out = pl.pallas_call(
        _ada_kernel,
        grid=(depth, d3 // cols),
        in_specs=[
            pl.BlockSpec((nb, d), lambda l, j: (0, 0)),
            pl.BlockSpec((None, d, cols), lambda l, j: (l, 0, j)),
            pl.BlockSpec((None, 1, cols), lambda l, j: (l, 0, j)),
        ],
        out_specs=pl.BlockSpec((None, nb, cols), lambda l, j: (l, 0, j)),
        out_shape=jax.ShapeDtypeStruct((depth, nb, d3), F32),
        compiler_params=_cparams(("arbitrary", "arbitrary")),
        name="ada_modulation",
    )(c, ada_w, ada_b.reshape(depth, 1, d3))
    return out.reshape(depth, nb, 3, d)


def _layer0_kernel(x_ref, mod0_ref, mod1_ref, g0_ref, g1_ref, win_ref, cw_ref, cb_ref, lg_ref, lb_ref,
                   wout_ref, perm_ref, x1_ref, h1_ref, h1c4_ref, h1c16_ref, ubuf, sbuf, cbuf):
    ts = x_ref.shape[0]
    d = x_ref.shape[1]

    @pl.when(pl.program_id(1) == 0)
    def _():
        ubuf[0:HALO, :] = jnp.zeros((HALO, d), F32)

    shift, scale, gate = mod0_ref[0:1, :], mod0_ref[1:2, :], mod0_ref[2:3, :]
    gain0 = g0_ref[...] * (1.0 + scale)
    off = HALO - (CONV_WIDTH - 1)
    shifted_rows = sbuf.shape[1]
    x = x_ref[...]
    ms = jnp.mean(x * x, axis=-1, keepdims=True)
    h = x * lax.rsqrt(ms + NORM_EPS) * gain0 + shift
    proj = jnp.dot(h.astype(BF16), win_ref[...], preferred_element_type=F32)
    u = proj[:, 0:d] * _sigmoid(proj[:, d:2 * d])
    sz = _silu(proj[:, 2 * d:3 * d])
    ubuf[HALO:HALO + ts, :] = u

    for s in range(1, SUBLANES):
        sbuf[s - 1] = ubuf[s:s + shifted_rows, :]
    for ls in range(d // LANES):
        lsl = slice(ls * LANES, (ls + 1) * LANES)
        for rc in range(ts // CONV_ROWS):
            r0 = rc * CONV_ROWS
            acc = jnp.broadcast_to(cb_ref[0:1, lsl], (CONV_ROWS, LANES))
            for j in range(CONV_WIDTH):
                s = (off + j) % SUBLANES
                a = r0 + off + j - s
                if s == 0:
                    tap = ubuf[a:a + CONV_ROWS, lsl]
                else:
                    tap = sbuf[s - 1, a:a + CONV_ROWS, lsl]
                acc = acc + cw_ref[j:j + 1, lsl] * tap
            cbuf[r0:r0 + CONV_ROWS, lsl] = acc
    ubuf[0:HALO, :] = ubuf[ts:ts + HALO, :]

    cv = cbuf[...]
    mu = jnp.mean(cv, axis=-1, keepdims=True)
    xc = cv - mu
    var = jnp.mean(xc * xc, axis=-1, keepdims=True)
    y = xc * lax.rsqrt(var + NORM_EPS) * lg_ref[...] + lb_ref[...]
    a = (_silu(y) * sz).astype(BF16)
    out = jnp.dot(a, wout_ref[...], preferred_element_type=F32)
    x1 = x + gate * out
    x1_ref[...] = x1

    shift1, scale1 = mod1_ref[0:1, :], mod1_ref[1:2, :]
    ms1 = jnp.mean(x1 * x1, axis=-1, keepdims=True)
    h1 = x1 * lax.rsqrt(ms1 + NORM_EPS) * (g1_ref[...] * (1.0 + scale1)) + shift1
    h1b = h1.astype(BF16)
    h1_ref[...] = h1b
    hp = jnp.dot(perm_ref[...], h1b, preferred_element_type=F32).astype(BF16)
    for k, ref in enumerate((h1c4_ref, h1c16_ref)):
        ref[...] = hp[k * ts:(k + 1) * ts].reshape(ref.shape)


def _layer0(x, mod, norm_g, w_in, conv_w, conv_b, ln_g, ln_b, w_out):
    nb, s, d = x.shape
    ts = L0_TILE
    dils = [dil for _, dil in DILATED_GROUPS if dil > 1]
    row = lambda a: a.reshape(1, d)
    const = lambda shape: pl.BlockSpec(shape, lambda b, t: (0,) * len(shape))
    perm = np.zeros((len(dils) * ts, ts), np.float32)
    for k, dil in enumerate(dils):
        for r in range(dil):
            for i in range(ts // dil):
                perm[k * ts + r * (ts // dil) + i, i * dil + r] = 1.0
    return pl.pallas_call(
        _layer0_kernel,
        grid=(nb, s // ts),
        in_specs=[
            pl.BlockSpec((None, ts, d), lambda b, t: (b, t, 0)),
            pl.BlockSpec((None, None, 3, d), lambda b, t: (0, b, 0, 0)),
            pl.BlockSpec((None, None, 3, d), lambda b, t: (1, b, 0, 0)),
            const((1, d)), const((1, d)),
            const((d, 3 * d)),
            const((CONV_WIDTH, d)), const((1, d)), const((1, d)), const((1, d)),
            const((d, d)), const(perm.shape),
        ],
        out_specs=[
            pl.BlockSpec((None, ts, d), lambda b, t: (b, t, 0)),
            pl.BlockSpec((None, ts, d), lambda b, t: (b, t, 0)),
        ] + [pl.BlockSpec((None, dil, ts // dil, d), lambda b, t: (b, 0, t, 0)) for dil in dils],
        out_shape=[jax.ShapeDtypeStruct((nb, s, d), F32), jax.ShapeDtypeStruct((nb, s, d), BF16)]
        + [jax.ShapeDtypeStruct((nb, dil, s // dil, d), BF16) for dil in dils],
        scratch_shapes=[pltpu.VMEM((HALO + ts, d), F32),
                        pltpu.VMEM((SUBLANES - 1, HALO + ts - SUBLANES, d), F32),
                        pltpu.VMEM((ts, d), F32)],
        compiler_params=_cparams(("arbitrary", "arbitrary")),
        name="layer0_conv_mixer",
    )(x, mod, mod, row(norm_g[0]), row(norm_g[1]), w_in.astype(BF16), conv_w, row(conv_b), row(ln_g),
      row(ln_b), w_out.astype(BF16), jnp.asarray(perm, BF16))


def _store_rows(ref, x):
    for p in range(N_PAIRS):
        ref[p] = x[:, p * LANES:(p + 1) * LANES]


def _store_cols(ref, xt):
    for tb in range(xt.shape[1] // BLOCK):
        for p in range(N_PAIRS):
            ref[p, tb] = xt[p * LANES:(p + 1) * LANES, tb * BLOCK:(tb + 1) * BLOCK]


def _proj_kernel(*refs, plain_transposed, with_z):
    if with_z:
        h_ref, wn_ref, wt_ref, wp_ref, gn_ref, gt_ref, wz_ref, n_ref, t_ref, p_ref, z_ref = refs
    else:
        h_ref, wn_ref, wt_ref, wp_ref, gn_ref, gt_ref, n_ref, t_ref, p_ref = refs
    h = h_ref[...]
    tm = h.shape[0]
    tn_dims = (((0,), (1,)), ((), ()))
    lane = lax.broadcasted_iota(jnp.int32, (tm, LANES), 1)
    low = lane < HEAD_DIM

    xn = jnp.dot(h, wn_ref[...].astype(BF16), preferred_element_type=F32)
    for p in range(N_PAIRS):
        xq = xn[:, p * LANES:(p + 1) * LANES]
        x2 = xq * xq
        s_lo = jnp.sum(jnp.where(low, x2, 0.0), axis=-1, keepdims=True)
        s_hi = jnp.sum(jnp.where(low, 0.0, x2), axis=-1, keepdims=True)
        inv = lax.rsqrt(jnp.where(low, s_lo, s_hi) * (1.0 / HEAD_DIM) + NORM_EPS)
        n_ref[p] = (xq * inv * gn_ref[...]).astype(BF16)

    xt = lax.dot_general(wt_ref[...].astype(BF16), h, tn_dims, preferred_element_type=F32)
    for tb in range(tm // BLOCK):
        blk = xt[:, tb * BLOCK:(tb + 1) * BLOCK].reshape(N_HEADS, HEAD_DIM, BLOCK)
        ss = jnp.sum(blk * blk, axis=1, keepdims=True)
        tn = blk * lax.rsqrt(ss * (1.0 / HEAD_DIM) + NORM_EPS) * gt_ref[...]
        tn = tn.reshape(N_PAIRS, 2 * HEAD_DIM, BLOCK).astype(BF16)
        for p in range(N_PAIRS):
            t_ref[p, tb] = tn[p]

    wp = wp_ref[...].astype(BF16)
    if plain_transposed:
        _store_cols(p_ref, lax.dot_general(wp, h, tn_dims, preferred_element_type=F32).astype(BF16))
    else:
        _store_rows(p_ref, jnp.dot(h, wp, preferred_element_type=F32).astype(BF16))
    if with_z:
        _store_rows(z_ref, jnp.dot(h, wz_ref[...].astype(BF16), preferred_element_type=F32).astype(BF16))


def _project_group(h, g, w_all, cols, gn, gt, plain_transposed):
    nb, s, d = h.shape
    tm = TOK_TILE
    steps = s // tm
    const = lambda shape: pl.BlockSpec(shape, lambda b, t: (0,) * len(shape))
    w_spec = lambda col: pl.BlockSpec((d, d), lambda b, t: (0, col), pipeline_mode=pl.Buffered(1))
    with_z = len(cols) == 4
    in_specs = [pl.BlockSpec((None, tm, d), lambda b, t: (b, t, 0)), w_spec(cols[0]), w_spec(cols[1]),
                w_spec(cols[2]), const((1, LANES)), const((N_HEADS, HEAD_DIM, BLOCK))]
    args = [h, w_all, w_all, w_all, gn, gt]
    pm_spec = pl.BlockSpec((None, N_PAIRS, tm, LANES), lambda b, t: (b, 0, t, 0))
    pm_shape = jax.ShapeDtypeStruct((nb, N_PAIRS, s, LANES), BF16)
    fm_spec = pl.BlockSpec((None, N_PAIRS, tm // BLOCK, LANES, BLOCK), lambda b, t: (b, 0, t, 0, 0))
    fm_shape = jax.ShapeDtypeStruct((nb, N_PAIRS, s // BLOCK, LANES, BLOCK), BF16)
    out_specs = [pm_spec, fm_spec, fm_spec if plain_transposed else pm_spec]
    out_shape = [pm_shape, fm_shape, fm_shape if plain_transposed else pm_shape]
    if with_z:
        in_specs.append(w_spec(cols[3]))
        args.append(w_all)
        out_specs.append(pm_spec)
        out_shape.append(pm_shape)
    return pl.pallas_call(
        functools.partial(_proj_kernel, plain_transposed=plain_transposed, with_z=with_z),
        grid=(nb, steps),
        in_specs=in_specs,
        out_specs=out_specs,
        out_shape=out_shape,
        compiler_params=_cparams(("arbitrary", "arbitrary")),
        name=f"proj_group{g}",
    )(*args)


def _alibi_slopes():
    return np.exp2(-8.0 * np.arange(1, N_HEADS + 1, dtype=np.float64) / N_HEADS)


def _bias_tables(dilation):
    kj = np.arange(2 * BLOCK)[:, None]
    qi = np.arange(BLOCK)[None, :]
    steps = qi + BLOCK - kj
    n_steps = BLOCK
    valid = (steps >= 0) & (steps <= n_steps)
    valid_first = valid & (kj >= BLOCK)
    dist = (steps * dilation).astype(np.float64)
    slopes = _alibi_slopes()
    gen = np.empty((2, N_PAIRS, 2 * BLOCK, 2 * BLOCK), np.float32)
    for p in range(N_PAIRS):
        for half in range(2):
            sl = slopes[2 * p + half]
            cols = slice(half * BLOCK, (half + 1) * BLOCK)
            gen[0, p, :, cols] = np.where(valid, -sl * dist, NEG_INF)
            gen[1, p, :, cols] = np.where(valid_first, -sl * dist, NEG_INF)
    first = np.ascontiguousarray(gen[1, :, BLOCK:, :])
    return gen, first


def _attn_banded_kernel(*refs, mode):
    if mode == "chain":
        q_ref, kt_ref, v_ref, ktp_ref, vp_ref, bias_ref, o_ref, st_ref = refs
    else:
        q_ref, kt_ref, v_ref, bias_ref, o_ref, st_ref = refs
    lane = lax.broadcasted_iota(jnp.int32, (BLOCK, LANES), 1)
    low = lane < HEAD_DIM
    ones_cols = jnp.ones((2 * BLOCK, LANES), BF16)
    is_start = (pl.program_id(1) == 0).astype(jnp.int32)
    for bi in range(Q_BLOCKS):
        rows = slice(bi * BLOCK, (bi + 1) * BLOCK)
        stats = jnp.ones((BLOCK, LANES), F32)
        for p in range(N_PAIRS):
            qb = q_ref[p, rows, :]
            zero = jnp.zeros_like(qb)
            q2 = jnp.concatenate([jnp.where(low, qb, zero), jnp.where(low, zero, qb)], axis=0)
            k_cur = kt_ref[p, bi]
            v_cur = v_ref[p, rows, :]
            if bi > 0:
                k_prev, v_prev, bias = kt_ref[p, bi - 1], v_ref[p, (bi - 1) * BLOCK:bi * BLOCK, :], bias_ref[0, p]
            elif mode == "chain":
                k_prev, v_prev, bias = ktp_ref[p, 0], vp_ref[p], bias_ref[is_start, p]
            else:
                k_prev, v_prev, bias = k_cur, v_cur, bias_ref[1, p]
            kk = jnp.concatenate([k_prev, k_cur], axis=1)
            vv = jnp.concatenate([jnp.concatenate([v_prev, v_cur], axis=0), ones_cols], axis=1)
            s = jnp.dot(q2, kk, preferred_element_type=F32) + bias
            m = jnp.max(s, axis=-1, keepdims=True)
            pe = jnp.exp(s - m).astype(BF16)
            r = jnp.dot(pe, vv, preferred_element_type=F32)
            o_ref[p, rows, :] = jnp.where(low, r[0:BLOCK, 0:LANES], r[BLOCK:2 * BLOCK, 0:LANES]).astype(BF16)
            stats = jnp.where(lane == 2 * p, m[0:BLOCK], stats)
            stats = jnp.where(lane == 2 * p + 1, m[BLOCK:2 * BLOCK], stats)
            stats = jnp.where(lane == N_HEADS + 2 * p, r[0:BLOCK, LANES:2 * LANES], stats)
            stats = jnp.where(lane == N_HEADS + 2 * p + 1, r[BLOCK:2 * BLOCK, LANES:2 * LANES], stats)
        st_ref[rows, :] = stats


def _attn_single_kernel(k_ref, qt_ref, vt_ref, bias_ref, o_ref, st_ref, stt):
    zeros = jnp.zeros((HEAD_DIM, BLOCK), BF16)
    stt[...] = jnp.ones(stt.shape, F32)
    for bi in range(Q_BLOCKS):
        rows = slice(bi * BLOCK, (bi + 1) * BLOCK)
        for p in range(N_PAIRS):
            qt = qt_ref[p, bi]
            q2t = jnp.concatenate(
                [jnp.concatenate([qt[0:HEAD_DIM], zeros], axis=0),
                 jnp.concatenate([zeros, qt[HEAD_DIM:2 * HEAD_DIM]], axis=0)], axis=1)
            st = jnp.dot(k_ref[p, rows, :], q2t, preferred_element_type=F32) + bias_ref[p]
            m = jnp.max(st, axis=0, keepdims=True)
            pe = jnp.exp(st - m)
            den = jnp.sum(pe, axis=0, keepdims=True)
            ot = jnp.dot(vt_ref[p, bi], pe.astype(BF16), preferred_element_type=F32)
            o_t = jnp.concatenate([ot[0:HEAD_DIM, 0:BLOCK], ot[HEAD_DIM:2 * HEAD_DIM, BLOCK:2 * BLOCK]], axis=0)
            o_ref[p, rows, :] = o_t.T.astype(BF16)
            stt[2 * p:2 * p + 1, :] = m[:, 0:BLOCK]
            stt[2 * p + 1:2 * p + 2, :] = m[:, BLOCK:2 * BLOCK]
            stt[N_HEADS + 2 * p:N_HEADS + 2 * p + 1, :] = den[:, 0:BLOCK]
            stt[N_HEADS + 2 * p + 1:N_HEADS + 2 * p + 2, :] = den[:, BLOCK:2 * BLOCK]
        st_ref[rows, :] = stt[...].T


def _attention_group(g, dilation, tok_major, feat_major, plain):
    nb, _, s, _ = tok_major.shape
    length = s // dilation
    tm = TOK_TILE
    gen_t, first_t = _bias_tables(dilation)
    const = lambda shape: pl.BlockSpec(shape, lambda b, t: (0,) * len(shape))
    pm_spec = pl.BlockSpec((None, N_PAIRS, tm, LANES), lambda b, t: (b, 0, t, 0))
    fm_spec = pl.BlockSpec((None, N_PAIRS, Q_BLOCKS, LANES, BLOCK), lambda b, t: (b, 0, t, 0, 0))
    scratch = []
    if length == BLOCK:
        body = _attn_single_kernel
        in_specs = [pm_spec, fm_spec, fm_spec, const(first_t.shape)]
        args = [tok_major, feat_major, plain, jnp.asarray(first_t)]
        scratch = [pltpu.VMEM((BLOCK, LANES), F32)]
    else:
        gen = np.ascontiguousarray(np.concatenate(
            [gen_t[..., 0:BLOCK].swapaxes(2, 3), gen_t[..., BLOCK:].swapaxes(2, 3)], axis=2))
        in_specs = [pm_spec, fm_spec, pm_spec]
        args = [tok_major, feat_major, plain]
        if length > tm:
            mode = "chain"
            prev = lambda t: jnp.maximum(t * Q_BLOCKS - 1, 0)
            in_specs += [
                pl.BlockSpec((None, N_PAIRS, 1, LANES, BLOCK), lambda b, t: (b, 0, prev(t), 0, 0)),
                pl.BlockSpec((None, N_PAIRS, BLOCK, LANES), lambda b, t: (b, 0, prev(t), 0)),
            ]
            args += [feat_major, plain]
        else:
            assert length == tm
            mode = "class"
        in_specs.append(const(gen.shape))
        args.append(jnp.asarray(gen))
        body = functools.partial(_attn_banded_kernel, mode=mode)
    return pl.pallas_call(
        body,
        grid=(nb, s // tm),
        in_specs=in_specs,
        out_specs=[pm_spec, pl.BlockSpec((None, tm, LANES), lambda b, t: (b, t, 0))],
        out_shape=[jax.ShapeDtypeStruct((nb, N_PAIRS, s, LANES), BF16), jax.ShapeDtypeStruct((nb, s, LANES), F32)],
        scratch_shapes=scratch,
        compiler_params=_cparams(("arbitrary", "arbitrary")),
        name=f"attn_group{g}",
    )(*args)


def _merge_kernel(o1_ref, o2_ref, o3_ref, s1_ref, s2_ref, s3_ref, z_ref, x_ref, mod_ref, e_ref, wout_ref,
                  out_ref, nat2, nat3, stn2, stn3):
    o_refs = [o1_ref, o2_ref, o3_ref]
    st_refs = [s1_ref, s2_ref, s3_ref]
    nat_bufs = [None, nat2, nat3]
    stn_bufs = [None, stn2, stn3]
    stats, ogs = [], []
    for g in range(3):
        dil, per_class = st_refs[g].shape[0], st_refs[g].shape[1]
        if dil == 1:
            stats.append(st_refs[g][0])
            ogs.append(jnp.concatenate([o_refs[g][p, 0] for p in range(N_PAIRS)], axis=1).astype(F32))
            continue
        for r in range(dil):
            dst = pl.ds(r, per_class, stride=dil)
            stn_bufs[g][dst, :] = st_refs[g][r]
            for p in range(N_PAIRS):
                nat_bufs[g][p, dst, :] = o_refs[g][p, r].astype(F32)
        stats.append(stn_bufs[g][...])
        ogs.append(jnp.concatenate([nat_bufs[g][p] for p in range(N_PAIRS)], axis=1))
    dens = [pltpu.roll(st, LANES - N_HEADS, axis=1) for st in stats]
    m = jnp.maximum(jnp.maximum(stats[0], stats[1]), stats[2])
    es = [jnp.exp(st - m) for st in stats]
    den = es[0] * dens[0] + es[1] * dens[1] + es[2] * dens[2]
    inv = 1.0 / den
    lane = lax.broadcasted_iota(jnp.int32, m.shape, 1)
    acc = None
    for g in range(3):
        w = jnp.where(lane < N_HEADS, es[g] * inv, 0.0)
        w_hi = w.astype(BF16)
        w_lo = (w - w_hi.astype(F32)).astype(BF16)
        wx = jnp.dot(jnp.concatenate([w_hi, w_lo], axis=1), e_ref[...], preferred_element_type=F32)
        acc = wx * ogs[g] if acc is None else acc + wx * ogs[g]
    z = jnp.concatenate([z_ref[p] for p in range(N_PAIRS)], axis=1).astype(F32)
    a = (acc * _silu(z)).astype(BF16)
    y = jnp.dot(a, wout_ref[...], preferred_element_type=F32)
    out_ref[...] = x_ref[...] + mod_ref[2:3, :] * y


def _merge(os_, stats, z, x1, mod, w_out):
    nb, s, d = x1.shape
    tm = TOK_TILE
    expand = np.zeros((2 * LANES, d), np.float32)
    for hd in range(N_HEADS):
        expand[hd, hd * HEAD_DIM:(hd + 1) * HEAD_DIM] = 1.0
        expand[LANES + hd, hd * HEAD_DIM:(hd + 1) * HEAD_DIM] = 1.0
    pm_spec = pl.BlockSpec((None, N_PAIRS, tm, LANES), lambda b, t: (b, 0, t, 0))
    x_spec = pl.BlockSpec((None, tm, d), lambda b, t: (b, t, 0))
    const = lambda shape: pl.BlockSpec(shape, lambda b, t: (0,) * len(shape))
    o_views, o_specs, st_views, st_specs = [], [], [], []
    for (_, dil), o, st in zip(DILATED_GROUPS, os_, stats):
        o_views.append(o.reshape(nb, N_PAIRS, dil, s // dil, LANES))
        o_specs.append(pl.BlockSpec((None, N_PAIRS, dil, tm // dil, LANES), lambda b, t: (b, 0, 0, t, 0)))
        st_views.append(st.reshape(nb, dil, s // dil, LANES))
        st_specs.append(pl.BlockSpec((None, dil, tm // dil, LANES), lambda b, t: (b, 0, t, 0)))
    return pl.pallas_call(
        _merge_kernel,
        grid=(nb, s // tm),
        in_specs=o_specs + st_specs + [
            pm_spec, x_spec, pl.BlockSpec((None, None, 3, d), lambda b, t: (1, b, 0, 0)),
            const((2 * LANES, d)), const((d, d))],
        out_specs=x_spec,
        out_shape=jax.ShapeDtypeStruct((nb, s, d), F32),
        scratch_shapes=[pltpu.VMEM((N_PAIRS, tm, LANES), F32), pltpu.VMEM((N_PAIRS, tm, LANES), F32),
                        pltpu.VMEM((tm, LANES), F32), pltpu.VMEM((tm, LANES), F32)],
        compiler_params=_cparams(("arbitrary", "arbitrary")),
        name="merge_out_proj",
    )(*o_views, *st_views, z, x1, mod, jnp.asarray(expand, BF16), w_out.astype(BF16))


def kernel(x, c, norm_g, ada_w, ada_b, a_w_in, a_conv_w, a_conv_b, a_ln_g, a_ln_b, a_w_out, b_w_in, b_q_norm,
           b_k_norm, b_w_out):
    d = D_MODEL
    s = x.shape[1]
    mod =_ada_modulation(c, ada_w, ada_b)
    x1, *h1s = _layer0(x, mod, norm_g, a_w_in[0], a_conv_w[0], a_conv_b[0], a_ln_g[0], a_ln_b[0], a_w_out[0])
    h1s = [h.reshape(x.shape) for h in h1s]

    w_b = b_w_in[0]
    z_col = 3 * len(DILATED_GROUPS)
    scale = HEAD_DIM ** -0.5
    os_, stats, z = [], [], None
    for g, (_, dilation) in enumerate(DILATED_GROUPS):
        q_col, k_col, v_col = 3 * g, 3 * g + 1, 3 * g + 2
        gq, gk = b_q_norm[0, g] * scale, b_k_norm[0, g]
        row_gain = lambda gain: jnp.tile(gain, 2).reshape(1, LANES)
        col_gain = lambda gain: jnp.broadcast_to(gain[None, :, None], (N_HEADS, HEAD_DIM, BLOCK))
        if s // dilation == BLOCK:
            outs = _project_group(h1s[g], g, w_b, (k_col, q_col, v_col), row_gain(gk), col_gain(gq), True)
        else:
            cols = (q_col, k_col, v_col) + ((z_col,) if g == 0 else ())
            outs = _project_group(h1s[g], g, w_b, cols, row_gain(gq), col_gain(gk), False)
        if g == 0:
            *outs, z = outs
        o, st = _attention_group(g, dilation, *outs)
        os_.append(o)
        stats.append(st)
    return _merge(os_, stats, z, x1, mod, b_w_out[0])
```

```python
import functools

import jax
import jax.numpy as jnp
import numpy as np
from jax import lax
from jax.experimental import pallas as pl
from jax.experimental.pallas import tpu as pltpu

F32 = jnp.float32
BF16 = jnp.bfloat16

D_MODEL = 1024
CONV_WIDTH = 31
HEAD_DIM = 64
N_HEADS = 16
N_PAIRS = N_HEADS // 2
DILATED_GROUPS = ((128, 1), (512, 4), (2048, 16))
BLOCK = 128
NORM_EPS = 1e-6
NEG_INF = -1e30
LANES = 128
SUBLANES = 8
VMEM_LIMIT = 56 * 1024 * 1024

TOK_TILE = 512
PROJ_TILE = 1024
ATTN_TILE = 1024
Q_BLOCKS = ATTN_TILE // BLOCK
L0_TILE = 512
HALO = 32
CONV_ROWS = 128


def _sigmoid(x):
    return jax.nn.sigmoid(x)


def _silu(x):
    return x * _sigmoid(x)


def _cparams(sem):
    return pltpu.CompilerParams(dimension_semantics=sem, vmem_limit_bytes=VMEM_LIMIT)


def _ada_kernel(c_ref, w_ref, b_ref, o_ref):
    c = c_ref[...]
    sc = _silu(c).astype(BF16)
    o_ref[...] = jnp.dot(sc, w_ref[...].astype(BF16), preferred_element_type=F32) + b_ref[...]


def _ada_modulation(c, ada_w, ada_b):
    depth, d, d3 = ada_w.shape
    nb = c.shape[0]
    cols = 768
    out = pl.pallas_call(
        _ada_kernel,
        grid=(depth, d3 // cols),
        in_specs=[
            pl.BlockSpec((nb, d), lambda l, j: (0, 0)),
            pl.BlockSpec((None, d, cols), lambda l, j: (l, 0, j)),
            pl.BlockSpec((None, 1, cols), lambda l, j: (l, 0, j)),
        ],
        out_specs=pl.BlockSpec((None, nb, cols), lambda l, j: (l, 0, j)),
        out_shape=jax.ShapeDtypeStruct((depth, nb, d3), F32),
        compiler_params=_cparams(("arbitrary", "arbitrary")),
        name="ada_modulation",
    )(c, ada_w, ada_b.reshape(depth, 1, d3))
    return out.reshape(depth, nb, 3, d)


def _layer0_kernel(x_ref, mod0_ref, mod1_ref, g0_ref, g1_ref, win_ref, cw_ref, cb_ref, lg_ref, lb_ref,
                   wout_ref, perm_ref, x1_ref, h1_ref, h1c4_ref, h1c16_ref, ubuf, sbuf, cbuf):
    ts = x_ref.shape[0]
    d = x_ref.shape[1]

    @pl.when(pl.program_id(1) == 0)
    def _():
        ubuf[0:HALO, :] = jnp.zeros((HALO, d), F32)

    shift, scale, gate = mod0_ref[0:1, :], mod0_ref[1:2, :], mod0_ref[2:3, :]
    gain0 = g0_ref[...] * (1.0 + scale)
    off = HALO - (CONV_WIDTH - 1)
    shifted_rows = sbuf.shape[1]
    x = x_ref[...]
    ms = jnp.mean(x * x, axis=-1, keepdims=True)
    h = x * lax.rsqrt(ms + NORM_EPS) * gain0 + shift
    proj = jnp.dot(h.astype(BF16), win_ref[...], preferred_element_type=F32)
    u = proj[:, 0:d] * _sigmoid(proj[:, d:2 * d])
    sz = _silu(proj[:, 2 * d:3 * d])
    ubuf[HALO:HALO + ts, :] = u

    for s in range(1, SUBLANES):
        sbuf[s - 1] = ubuf[s:s + shifted_rows, :]
    for ls in range(d // LANES):
        lsl = slice(ls * LANES, (ls + 1) * LANES)
        for rc in range(ts // CONV_ROWS):
            r0 = rc * CONV_ROWS
            acc = jnp.broadcast_to(cb_ref[0:1, lsl], (CONV_ROWS, LANES))
            for j in range(CONV_WIDTH):
                s = (off + j) % SUBLANES
                a = r0 + off + j - s
                if s == 0:
                    tap = ubuf[a:a + CONV_ROWS, lsl]
                else:
                    tap = sbuf[s - 1, a:a + CONV_ROWS, lsl]
                acc = acc + cw_ref[j:j + 1, lsl] * tap
            cbuf[r0:r0 + CONV_ROWS, lsl] = acc
    ubuf[0:HALO, :] = ubuf[ts:ts + HALO, :]

    cv = cbuf[...]
    mu = jnp.mean(cv, axis=-1, keepdims=True)
    xc = cv - mu
    var = jnp.mean(xc * xc, axis=-1, keepdims=True)
    y = xc * lax.rsqrt(var + NORM_EPS) * lg_ref[...] + lb_ref[...]
    a = (_silu(y) * sz).astype(BF16)
    out = jnp.dot(a, wout_ref[...], preferred_element_type=F32)
    x1 = x + gate * out
    x1_ref[...] = x1

    shift1, scale1 = mod1_ref[0:1, :], mod1_ref[1:2, :]
    ms1 = jnp.mean(x1 * x1, axis=-1, keepdims=True)
    h1 = x1 * lax.rsqrt(ms1 + NORM_EPS) * (g1_ref[...] * (1.0 + scale1)) + shift1
    h1b = h1.astype(BF16)
    h1_ref[...] = h1b
    hp = jnp.dot(perm_ref[...], h1b, preferred_element_type=F32).astype(BF16)
    for k, ref in enumerate((h1c4_ref, h1c16_ref)):
        ref[...] = hp[k * ts:(k + 1) * ts].reshape(ref.shape)


def _layer0(x, mod, norm_g, w_in, conv_w, conv_b, ln_g, ln_b, w_out):
    nb, s, d = x.shape
    ts = L0_TILE
    dils = [dil for _, dil in DILATED_GROUPS if dil > 1]
    row = lambda a: a.reshape(1, d)
    const = lambda shape: pl.BlockSpec(shape, lambda b, t: (0,) * len(shape))
    once = lambda shape: pl.BlockSpec(shape, lambda b, t: (0,) * len(shape), pipeline_mode=pl.Buffered(1))
    perm = np.zeros((len(dils) * ts, ts), np.float32)
    for k, dil in enumerate(dils):
        for r in range(dil):
            for i in range(ts // dil):
                perm[k * ts + r * (ts // dil) + i, i * dil + r] = 1.0
    return pl.pallas_call(
        _layer0_kernel,
        grid=(nb, s // ts),
        in_specs=[
            pl.BlockSpec((None, ts, d), lambda b, t: (b, t, 0)),
            pl.BlockSpec((None, None, 3, d), lambda b, t: (0, b, 0, 0)),
            pl.BlockSpec((None, None, 3, d), lambda b, t: (1, b, 0, 0)),
            const((1, d)), const((1, d)),
            once((d, 3 * d)),
            const((CONV_WIDTH, d)), const((1, d)), const((1, d)), const((1, d)),
            once((d, d)), once(perm.shape),
        ],
        out_specs=[
            pl.BlockSpec((None, ts, d), lambda b, t: (b, t, 0)),
            pl.BlockSpec((None, ts, d), lambda b, t: (b, t, 0)),
        ] + [pl.BlockSpec((None, dil, ts // dil, d), lambda b, t: (b, 0, t, 0)) for dil in dils],
        out_shape=[jax.ShapeDtypeStruct((nb, s, d), F32), jax.ShapeDtypeStruct((nb, s, d), BF16)]
        + [jax.ShapeDtypeStruct((nb, dil, s // dil, d), BF16) for dil in dils],
        scratch_shapes=[pltpu.VMEM((HALO + ts, d), F32),
                        pltpu.VMEM((SUBLANES - 1, HALO + ts - SUBLANES, d), F32),
                        pltpu.VMEM((ts, d), F32)],
        compiler_params=_cparams(("arbitrary", "arbitrary")),
        name="layer0_conv_mixer",
    )(x, mod, mod, row(norm_g[0]), row(norm_g[1]), w_in.astype(BF16), conv_w, row(conv_b), row(ln_g),
      row(ln_b), w_out.astype(BF16), jnp.asarray(perm, BF16))


def _store_rows(ref, x):
    for p in range(N_PAIRS):
        ref[p] = x[:, p * LANES:(p + 1) * LANES]


def _store_cols(ref, xt):
    for tb in range(xt.shape[1] // BLOCK):
        for p in range(N_PAIRS):
            ref[p, tb] = xt[p * LANES:(p + 1) * LANES, tb * BLOCK:(tb + 1) * BLOCK]


def _proj_kernel(*refs, plain_transposed, with_z):
    if with_z:
        h_ref, wn_ref, wt_ref, wp_ref, gn_ref, gt_ref, wz_ref, n_ref, t_ref, p_ref, z_ref = refs
    else:
        h_ref, wn_ref, wt_ref, wp_ref, gn_ref, gt_ref, n_ref, t_ref, p_ref = refs
    h = h_ref[...]
    tm = h.shape[0]
    tn_dims = (((0,), (1,)), ((), ()))
    lane = lax.broadcasted_iota(jnp.int32, (tm, LANES), 1)
    low = lane < HEAD_DIM

    xn = jnp.dot(h, wn_ref[...].astype(BF16), preferred_element_type=F32)
    for p in range(N_PAIRS):
        xq = xn[:, p * LANES:(p + 1) * LANES]
        x2 = xq * xq
        s_lo = jnp.sum(jnp.where(low, x2, 0.0), axis=-1, keepdims=True)
        s_hi = jnp.sum(jnp.where(low, 0.0, x2), axis=-1, keepdims=True)
        inv = lax.rsqrt(jnp.where(low, s_lo, s_hi) * (1.0 / HEAD_DIM) + NORM_EPS)
        n_ref[p] = (xq * inv * gn_ref[...]).astype(BF16)

    xt = lax.dot_general(wt_ref[...].astype(BF16), h, tn_dims, preferred_element_type=F32)
    for tb in range(tm // BLOCK):
        blk = xt[:, tb * BLOCK:(tb + 1) * BLOCK].reshape(N_HEADS, HEAD_DIM, BLOCK)
        ss = jnp.sum(blk * blk, axis=1, keepdims=True)
        tn = blk * lax.rsqrt(ss * (1.0 / HEAD_DIM) + NORM_EPS) * gt_ref[...]
        tn = tn.reshape(N_PAIRS, 2 * HEAD_DIM, BLOCK).astype(BF16)
        for p in range(N_PAIRS):
            t_ref[p, tb] = tn[p]

    wp = wp_ref[...].astype(BF16)
    if plain_transposed:
        _store_cols(p_ref, lax.dot_general(wp, h, tn_dims, preferred_element_type=F32).astype(BF16))
    else:
        _store_rows(p_ref, jnp.dot(h, wp, preferred_element_type=F32).astype(BF16))
    if with_z:
        _store_rows(z_ref, jnp.dot(h, wz_ref[...].astype(BF16), preferred_element_type=F32).astype(BF16))


def _project_group(h, g, w_all, cols, gn, gt, plain_transposed):
    nb, s, d = h.shape
    tm = PROJ_TILE
    steps = s // tm
    const = lambda shape: pl.BlockSpec(shape, lambda b, t: (0,) * len(shape))
    w_spec = lambda col: pl.BlockSpec((d, d), lambda b, t: (0, col), pipeline_mode=pl.Buffered(1))
    with_z = len(cols) == 4
    in_specs = [pl.BlockSpec((None, tm, d), lambda b, t: (b, t, 0)), w_spec(cols[0]), w_spec(cols[1]),
                w_spec(cols[2]), const((1, LANES)), const((N_HEADS, HEAD_DIM, BLOCK))]
    args = [h, w_all, w_all, w_all, gn, gt]
    pm_spec = pl.BlockSpec((None, N_PAIRS, tm, LANES), lambda b, t: (b, 0, t, 0))
    pm_shape = jax.ShapeDtypeStruct((nb, N_PAIRS, s, LANES), BF16)
    fm_spec = pl.BlockSpec((None, N_PAIRS, tm // BLOCK, LANES, BLOCK), lambda b, t: (b, 0, t, 0, 0))
    fm_shape = jax.ShapeDtypeStruct((nb, N_PAIRS, s // BLOCK, LANES, BLOCK), BF16)
    out_specs = [pm_spec, fm_spec, fm_spec if plain_transposed else pm_spec]
    out_shape = [pm_shape, fm_shape, fm_shape if plain_transposed else pm_shape]
    if with_z:
        in_specs.append(w_spec(cols[3]))
        args.append(w_all)
        out_specs.append(pm_spec)
        out_shape.append(pm_shape)
    return pl.pallas_call(
        functools.partial(_proj_kernel, plain_transposed=plain_transposed, with_z=with_z),
        grid=(nb, steps),
        in_specs=in_specs,
        out_specs=out_specs,
        out_shape=out_shape,
        compiler_params=_cparams(("arbitrary", "arbitrary")),
        name=f"proj_group{g}",
    )(*args)


def _alibi_slopes():
    return np.exp2(-8.0 * np.arange(1, N_HEADS + 1, dtype=np.float64) / N_HEADS)


def _bias_tables(dilation):
    kj = np.arange(2 * BLOCK)[:, None]
    qi = np.arange(BLOCK)[None, :]
    steps = qi + BLOCK - kj
    n_steps = BLOCK
    valid = (steps >= 0) & (steps <= n_steps)
    valid_first = valid & (kj >= BLOCK)
    dist = (steps * dilation).astype(np.float64)
    slopes = _alibi_slopes()
    gen = np.empty((2, N_PAIRS, 2 * BLOCK, 2 * BLOCK), np.float32)
    for p in range(N_PAIRS):
        for half in range(2):
            sl = slopes[2 * p + half]
            cols = slice(half * BLOCK, (half + 1) * BLOCK)
            gen[0, p, :, cols] = np.where(valid, -sl * dist, NEG_INF)
            gen[1, p, :, cols] = np.where(valid_first, -sl * dist, NEG_INF)
    first = np.ascontiguousarray(gen[1, :, BLOCK:, :])
    return gen, first


def _attn_banded_kernel(*refs, mode, seq_blocks):
    if mode == "chain":
        q_ref, kt_ref, v_ref, ktp_ref, vp_ref, bias_ref, o_ref, st_ref = refs
    else:
        q_ref, kt_ref, v_ref, bias_ref, o_ref, st_ref = refs
    lane = lax.broadcasted_iota(jnp.int32, (BLOCK, LANES), 1)
    low = lane < HEAD_DIM
    ones_cols = jnp.ones((2 * BLOCK, LANES), BF16)
    is_start = (pl.program_id(1) == 0).astype(jnp.int32)
    for bi in range(Q_BLOCKS):
        rows = slice(bi * BLOCK, (bi + 1) * BLOCK)
        stats = jnp.ones((BLOCK, LANES), F32)
        for p in range(N_PAIRS):
            qb = q_ref[p, rows, :]
            zero = jnp.zeros_like(qb)
            q2 = jnp.concatenate([jnp.where(low, qb, zero), jnp.where(low, zero, qb)], axis=0)
            k_cur = kt_ref[p, bi]
            v_cur = v_ref[p, rows, :]
            if bi % seq_blocks > 0:
                k_prev, v_prev, bias = kt_ref[p, bi - 1], v_ref[p, (bi - 1) * BLOCK:bi * BLOCK, :], bias_ref[0, p]
            elif mode == "chain":
                k_prev, v_prev, bias = ktp_ref[p, 0], vp_ref[p], bias_ref[is_start, p]
            else:
                k_prev, v_prev, bias = k_cur, v_cur, bias_ref[1, p]
            kk = jnp.concatenate([k_prev, k_cur], axis=1)
            vv = jnp.concatenate([jnp.concatenate([v_prev, v_cur], axis=0), ones_cols], axis=1)
            s = jnp.dot(q2, kk, preferred_element_type=F32) + bias
            m = jnp.max(s, axis=-1, keepdims=True)
            pe = jnp.exp(s - m).astype(BF16)
            r = jnp.dot(pe, vv, preferred_element_type=F32)
            o_ref[p, rows, :] = jnp.where(low, r[0:BLOCK, 0:LANES], r[BLOCK:2 * BLOCK, 0:LANES]).astype(BF16)
            stats = jnp.where(lane == 2 * p, m[0:BLOCK], stats)
            stats = jnp.where(lane == 2 * p + 1, m[BLOCK:2 * BLOCK], stats)
            stats = jnp.where(lane == N_HEADS + 2 * p, r[0:BLOCK, LANES:2 * LANES], stats)
            stats = jnp.where(lane == N_HEADS + 2 * p + 1, r[BLOCK:2 * BLOCK, LANES:2 * LANES], stats)
        st_ref[rows, :] = stats


def _attn_single_kernel(k_ref, qt_ref, vt_ref, bias_ref, o_ref, st_ref, stt):
    zeros = jnp.zeros((HEAD_DIM, BLOCK), BF16)
    stt[...] = jnp.ones(stt.shape, F32)
    for bi in range(Q_BLOCKS):
        rows = slice(bi * BLOCK, (bi + 1) * BLOCK)
        for p in range(N_PAIRS):
            qt = qt_ref[p, bi]
            q2t = jnp.concatenate(
                [jnp.concatenate([qt[0:HEAD_DIM], zeros], axis=0),
                 jnp.concatenate([zeros, qt[HEAD_DIM:2 * HEAD_DIM]], axis=0)], axis=1)
            st = jnp.dot(k_ref[p, rows, :], q2t, preferred_element_type=F32) + bias_ref[p]
            m = jnp.max(st, axis=0, keepdims=True)
            pe = jnp.exp(st - m)
            den = jnp.sum(pe, axis=0, keepdims=True)
            ot = jnp.dot(vt_ref[p, bi], pe.astype(BF16), preferred_element_type=F32)
            o_t = jnp.concatenate([ot[0:HEAD_DIM, 0:BLOCK], ot[HEAD_DIM:2 * HEAD_DIM, BLOCK:2 * BLOCK]], axis=0)
            o_ref[p, rows, :] = o_t.T.astype(BF16)
            stt[2 * p:2 * p + 1, :] = m[:, 0:BLOCK]
            stt[2 * p + 1:2 * p + 2, :] = m[:, BLOCK:2 * BLOCK]
            stt[N_HEADS + 2 * p:N_HEADS + 2 * p + 1, :] = den[:, 0:BLOCK]
            stt[N_HEADS + 2 * p + 1:N_HEADS + 2 * p + 2, :] = den[:, BLOCK:2 * BLOCK]
        st_ref[rows, :] = stt[...].T


def _attention_group(g, dilation, tok_major, feat_major, plain):
    nb, _, s, _ = tok_major.shape
    length = s // dilation
    tm = ATTN_TILE
    gen_t, first_t = _bias_tables(dilation)
    const = lambda shape: pl.BlockSpec(shape, lambda b, t: (0,) * len(shape))
    pm_spec = pl.BlockSpec((None, N_PAIRS, tm, LANES), lambda b, t: (b, 0, t, 0))
    fm_spec = pl.BlockSpec((None, N_PAIRS, Q_BLOCKS, LANES, BLOCK), lambda b, t: (b, 0, t, 0, 0))
    scratch = []
    if length == BLOCK:
        body = _attn_single_kernel
        in_specs = [pm_spec, fm_spec, fm_spec, const(first_t.shape)]
        args = [tok_major, feat_major, plain, jnp.asarray(first_t)]
        scratch = [pltpu.VMEM((BLOCK, LANES), F32)]
    else:
        gen = np.ascontiguousarray(np.concatenate(
            [gen_t[..., 0:BLOCK].swapaxes(2, 3), gen_t[..., BLOCK:].swapaxes(2, 3)], axis=2))
        in_specs = [pm_spec, fm_spec, pm_spec]
        args = [tok_major, feat_major, plain]
        if length > tm:
            mode = "chain"
            prev = lambda t: jnp.maximum(t * Q_BLOCKS - 1, 0)
            in_specs += [
                pl.BlockSpec((None, N_PAIRS, 1, LANES, BLOCK), lambda b, t: (b, 0, prev(t), 0, 0)),
                pl.BlockSpec((None, N_PAIRS, BLOCK, LANES), lambda b, t: (b, 0, prev(t), 0)),
            ]
            args += [feat_major, plain]
        else:
            assert tm % length == 0
            mode = "class"
        in_specs.append(const(gen.shape))
        args.append(jnp.asarray(gen))
        body = functools.partial(_attn_banded_kernel, mode=mode, seq_blocks=min(length, tm) // BLOCK)
    return pl.pallas_call(
        body,
        grid=(nb, s // tm),
        in_specs=in_specs,
        out_specs=[pm_spec, pl.BlockSpec((None, tm, LANES), lambda b, t: (b, t, 0))],
        out_shape=[jax.ShapeDtypeStruct((nb, N_PAIRS, s, LANES), BF16), jax.ShapeDtypeStruct((nb, s, LANES), F32)],
        scratch_shapes=scratch,
        compiler_params=_cparams(("arbitrary", "arbitrary")),
        name=f"attn_group{g}",
    )(*args)


def _merge_kernel(o1_ref, o2_ref, o3_ref, s1_ref, s2_ref, s3_ref, z_ref, x_ref, mod_ref, e_ref, wout_ref,
                  out_ref, nat2, nat3, stn2, stn3):
    o_refs = [o1_ref, o2_ref, o3_ref]
    st_refs = [s1_ref, s2_ref, s3_ref]
    nat_bufs = [None, nat2, nat3]
    stn_bufs = [None, stn2, stn3]
    stats, ogs = [], []
    for g in range(3):
        dil, per_class = st_refs[g].shape[0], st_refs[g].shape[1]
        if dil == 1:
            stats.append(st_refs[g][0])
            ogs.append(jnp.concatenate([o_refs[g][p, 0] for p in range(N_PAIRS)], axis=1).astype(F32))
            continue
        for r in range(dil):
            dst = pl.ds(r, per_class, stride=dil)
            stn_bufs[g][dst, :] = st_refs[g][r]
            for p in range(N_PAIRS):
                nat_bufs[g][p, dst, :] = o_refs[g][p, r].astype(F32)
        stats.append(stn_bufs[g][...])
        ogs.append(jnp.concatenate([nat_bufs[g][p] for p in range(N_PAIRS)], axis=1))
    dens = [pltpu.roll(st, LANES - N_HEADS, axis=1) for st in stats]
    m = jnp.maximum(jnp.maximum(stats[0], stats[1]), stats[2])
    es = [jnp.exp(st - m) for st in stats]
    den = es[0] * dens[0] + es[1] * dens[1] + es[2] * dens[2]
    inv = 1.0 / den
    lane = lax.broadcasted_iota(jnp.int32, m.shape, 1)
    acc = None
    for g in range(3):
        w = jnp.where(lane < N_HEADS, es[g] * inv, 0.0)
        w_hi = w.astype(BF16)
        w_lo = (w - w_hi.astype(F32)).astype(BF16)
        wx = jnp.dot(jnp.concatenate([w_hi, w_lo], axis=1), e_ref[...], preferred_element_type=F32)
        acc = wx * ogs[g] if acc is None else acc + wx * ogs[g]
    z = jnp.concatenate([z_ref[p] for p in range(N_PAIRS)], axis=1).astype(F32)
    a = (acc * _silu(z)).astype(BF16)
    y = jnp.dot(a, wout_ref[...], preferred_element_type=F32)
    out_ref[...] = x_ref[...] + mod_ref[2:3, :] * y


def _merge(os_, stats, z, x1, mod, w_out):
    nb, s, d = x1.shape
    tm = TOK_TILE
    expand = np.zeros((2 * LANES, d), np.float32)
    for hd in range(N_HEADS):
        expand[hd, hd * HEAD_DIM:(hd + 1) * HEAD_DIM] = 1.0
        expand[LANES + hd, hd * HEAD_DIM:(hd + 1) * HEAD_DIM] = 1.0
    pm_spec = pl.BlockSpec((None, N_PAIRS, tm, LANES), lambda b, t: (b, 0, t, 0))
    x_spec = pl.BlockSpec((None, tm, d), lambda b, t: (b, t, 0))
    const = lambda shape: pl.BlockSpec(shape, lambda b, t: (0,) * len(shape))
    o_views, o_specs, st_views, st_specs = [], [], [], []
    for (_, dil), o, st in zip(DILATED_GROUPS, os_, stats):
        o_views.append(o.reshape(nb, N_PAIRS, dil, s // dil, LANES))
        o_specs.append(pl.BlockSpec((None, N_PAIRS, dil, tm // dil, LANES), lambda b, t: (b, 0, 0, t, 0)))
        st_views.append(st.reshape(nb, dil, s // dil, LANES))
        st_specs.append(pl.BlockSpec((None, dil, tm // dil, LANES), lambda b, t: (b, 0, t, 0)))
    return pl.pallas_call(
        _merge_kernel,
        grid=(nb, s // tm),
        in_specs=o_specs + st_specs + [
            pm_spec, x_spec, pl.BlockSpec((None, None, 3, d), lambda b, t: (1, b, 0, 0)),
            const((2 * LANES, d)), const((d, d))],
        out_specs=x_spec,
        out_shape=jax.ShapeDtypeStruct((nb, s, d), F32),
        scratch_shapes=[pltpu.VMEM((N_PAIRS, tm, LANES), F32), pltpu.VMEM((N_PAIRS, tm, LANES), F32),
                        pltpu.VMEM((tm, LANES), F32), pltpu.VMEM((tm, LANES), F32)],
        compiler_params=_cparams(("arbitrary", "arbitrary")),
        name="merge_out_proj",
    )(*o_views, *st_views, z, x1, mod, jnp.asarray(expand, BF16), w_out.astype(BF16))


def kernel(x, c, norm_g, ada_w, ada_b, a_w_in, a_conv_w, a_conv_b, a_ln_g, a_ln_b, a_w_out, b_w_in, b_q_norm,
           b_k_norm, b_w_out):
    d = D_MODEL
    s = x.shape[1]
    mod =_ada_modulation(c, ada_w, ada_b)
    x1, *h1s = _layer0(x, mod, norm_g, a_w_in[0], a_conv_w[0], a_conv_b[0], a_ln_g[0], a_ln_b[0], a_w_out[0])
    h1s = [h.reshape(x.shape) for h in h1s]

    w_b = b_w_in[0]
    z_col = 3 * len(DILATED_GROUPS)
    scale = HEAD_DIM ** -0.5
    os_, stats, z = [], [], None
    for g, (_, dilation) in enumerate(DILATED_GROUPS):
        q_col, k_col, v_col = 3 * g, 3 * g + 1, 3 * g + 2
        gq, gk = b_q_norm[0, g] * scale, b_k_norm[0, g]
        row_gain = lambda gain: jnp.tile(gain, 2).reshape(1, LANES)
        col_gain = lambda gain: jnp.broadcast_to(gain[None, :, None], (N_HEADS, HEAD_DIM, BLOCK))
        if s // dilation == BLOCK:
            outs = _project_group(h1s[g], g, w_b, (k_col, q_col, v_col), row_gain(gk), col_gain(gq), True)
        else:
            cols = (q_col, k_col, v_col) + ((z_col,) if g == 0 else ())
            outs = _project_group(h1s[g], g, w_b, cols, row_gain(gq), col_gain(gk), False)
        if g == 0:
            *outs, z = outs
        o, st = _attention_group(g, dilation, *outs)
        os_.append(o)
        stats.append(st)
    return _merge(os_, stats, z, x1, mod, b_w_out[0])
```

```python
import functools

import jax
import jax.numpy as jnp
import numpy as np
from jax import lax
from jax.experimental import pallas as pl
from jax.experimental.pallas import tpu as pltpu

F32 = jnp.float32
BF16 = jnp.bfloat16

D_MODEL = 1024
CONV_WIDTH = 31
HEAD_DIM = 64
N_HEADS = 16
N_PAIRS = N_HEADS // 2
DILATED_GROUPS = ((128, 1), (512, 4), (2048, 16))
BLOCK = 128
NORM_EPS = 1e-6
NEG_INF = -1e30
LANES = 128
SUBLANES = 8
VMEM_LIMIT = 56 * 1024 * 1024

TOK_TILE = 512
UNPERM_ROWS = 256
PROJ_TILE = 1024
ATTN_TILE = 1024
Q_BLOCKS = ATTN_TILE // BLOCK
L0_TILE = 512
HALO = 32
CONV_ROWS = 128


def _sigmoid(x):
    return jax.nn.sigmoid(x)


def _silu(x):
    return x * _sigmoid(x)


def _cparams(sem):
    return pltpu.CompilerParams(dimension_semantics=sem, vmem_limit_bytes=VMEM_LIMIT)


def _ada_kernel(c_ref, w_ref, b_ref, o_ref):
    c = c_ref[...]
    sc = _silu(c).astype(BF16)
    o_ref[...] = jnp.dot(sc, w_ref[...].astype(BF16), preferred_element_type=F32) + b_ref[...]


def _ada_modulation(c, ada_w, ada_b):
    depth, d, d3 = ada_w.shape
    nb = c.shape[0]
    cols = d3 // 2
    out = pl.pallas_call(
        _ada_kernel,
        grid=(depth, d3 // cols),
        in_specs=[
            pl.BlockSpec((nb, d), lambda l, j: (0, 0)),
            pl.BlockSpec((None, d, cols), lambda l, j: (l, 0, j)),
            pl.BlockSpec((None, 1, cols), lambda l, j: (l, 0, j)),
        ],
        out_specs=pl.BlockSpec((None, nb, cols), lambda l, j: (l, 0, j)),
        out_shape=jax.ShapeDtypeStruct((depth, nb, d3), F32),
        compiler_params=_cparams(("arbitrary", "arbitrary")),
        name="ada_modulation",
    )(c, ada_w, ada_b.reshape(depth, 1, d3))
    return out.reshape(depth, nb, 3, d)


def _layer0_kernel(x_ref, mod0_ref, mod1_ref, g0_ref, g1_ref, win_ref, cw_ref, cb_ref, lg_ref, lb_ref,
                   wout_ref, perm_ref, x1_ref, h1_ref, h1c4_ref, h1c16_ref, ubuf, sbuf, cbuf):
    ts = x_ref.shape[0]
    d = x_ref.shape[1]

    @pl.when(pl.program_id(1) == 0)
    def _():
        ubuf[0:HALO, :] = jnp.zeros((HALO, d), F32)

    shift, scale, gate = mod0_ref[0:1, :], mod0_ref[1:2, :], mod0_ref[2:3, :]
    gain0 = g0_ref[...] * (1.0 + scale)
    off = HALO - (CONV_WIDTH - 1)
    shifted_rows = sbuf.shape[1]
    x = x_ref[...]
    ms = jnp.mean(x * x, axis=-1, keepdims=True)
    h = x * lax.rsqrt(ms + NORM_EPS) * gain0 + shift
    proj = jnp.dot(h.astype(BF16), win_ref[...], preferred_element_type=F32)
    u = proj[:, 0:d] * _sigmoid(proj[:, d:2 * d])
    sz = _silu(proj[:, 2 * d:3 * d])
    ubuf[HALO:HALO + ts, :] = u

    for s in range(1, SUBLANES):
        sbuf[s - 1] = ubuf[s:s + shifted_rows, :]
    for ls in range(d // LANES):
        lsl = slice(ls * LANES, (ls + 1) * LANES)
        for rc in range(ts // CONV_ROWS):
            r0 = rc * CONV_ROWS
            acc = jnp.broadcast_to(cb_ref[0:1, lsl], (CONV_ROWS, LANES))
            for j in range(CONV_WIDTH):
                s = (off + j) % SUBLANES
                a = r0 + off + j - s
                if s == 0:
                    tap = ubuf[a:a + CONV_ROWS, lsl]
                else:
                    tap = sbuf[s - 1, a:a + CONV_ROWS, lsl]
                acc = acc + cw_ref[j:j + 1, lsl] * tap
            cbuf[r0:r0 + CONV_ROWS, lsl] = acc
    ubuf[0:HALO, :] = ubuf[ts:ts + HALO, :]

    cv = cbuf[...]
    mu = jnp.mean(cv, axis=-1, keepdims=True)
    xc = cv - mu
    var = jnp.mean(xc * xc, axis=-1, keepdims=True)
    y = xc * lax.rsqrt(var + NORM_EPS) * lg_ref[...] + lb_ref[...]
    a = (_silu(y) * sz).astype(BF16)
    out = jnp.dot(a, wout_ref[...], preferred_element_type=F32)
    x1 = x + gate * out
    x1_ref[...] = x1

    shift1, scale1 = mod1_ref[0:1, :], mod1_ref[1:2, :]
    ms1 = jnp.mean(x1 * x1, axis=-1, keepdims=True)
    h1 = x1 * lax.rsqrt(ms1 + NORM_EPS) * (g1_ref[...] * (1.0 + scale1)) + shift1
    h1b = h1.astype(BF16)
    h1_ref[...] = h1b
    hp = jnp.dot(perm_ref[...], h1b, preferred_element_type=F32).astype(BF16)
    for k, ref in enumerate((h1c4_ref, h1c16_ref)):
        ref[...] = hp[k * ts:(k + 1) * ts].reshape(ref.shape)


def _layer0(x, mod, norm_g, w_in, conv_w, conv_b, ln_g, ln_b, w_out):
    nb, s, d = x.shape
    ts = L0_TILE
    dils = [dil for _, dil in DILATED_GROUPS if dil > 1]
    row = lambda a: a.reshape(1, d)
    const = lambda shape: pl.BlockSpec(shape, lambda b, t: (0,) * len(shape))
    once = lambda shape: pl.BlockSpec(shape, lambda b, t: (0,) * len(shape), pipeline_mode=pl.Buffered(1))
    perm = np.zeros((len(dils) * ts, ts), np.float32)
    for k, dil in enumerate(dils):
        for r in range(dil):
            for i in range(ts // dil):
                perm[k * ts + r * (ts // dil) + i, i * dil + r] = 1.0
    return pl.pallas_call(
        _layer0_kernel,
        grid=(nb, s // ts),
        in_specs=[
            pl.BlockSpec((None, ts, d), lambda b, t: (b, t, 0)),
            pl.BlockSpec((None, None, 3, d), lambda b, t: (0, b, 0, 0)),
            pl.BlockSpec((None, None, 3, d), lambda b, t: (1, b, 0, 0)),
            const((1, d)), const((1, d)),
            once((d, 3 * d)),
            const((CONV_WIDTH, d)), const((1, d)), const((1, d)), const((1, d)),
            once((d, d)), once(perm.shape),
        ],
        out_specs=[
            pl.BlockSpec((None, ts, d), lambda b, t: (b, t, 0)),
            pl.BlockSpec((None, ts, d), lambda b, t: (b, t, 0)),
        ] + [pl.BlockSpec((None, dil, ts // dil, d), lambda b, t: (b, 0, t, 0)) for dil in dils],
        out_shape=[jax.ShapeDtypeStruct((nb, s, d), F32), jax.ShapeDtypeStruct((nb, s, d), BF16)]
        + [jax.ShapeDtypeStruct((nb, dil, s // dil, d), BF16) for dil in dils],
        scratch_shapes=[pltpu.VMEM((HALO + ts, d), F32),
                        pltpu.VMEM((SUBLANES - 1, HALO + ts - SUBLANES, d), F32),
                        pltpu.VMEM((ts, d), F32)],
        compiler_params=_cparams(("arbitrary", "arbitrary")),
        name="layer0_conv_mixer",
    )(x, mod, mod, row(norm_g[0]), row(norm_g[1]), w_in.astype(BF16), conv_w, row(conv_b), row(ln_g),
      row(ln_b), w_out.astype(BF16), jnp.asarray(perm, BF16))


def _store_rows(ref, x):
    for p in range(N_PAIRS):
        ref[p] = x[:, p * LANES:(p + 1) * LANES]


def _store_cols(ref, xt):
    for tb in range(xt.shape[1] // BLOCK):
        for p in range(N_PAIRS):
            ref[p, tb] = xt[p * LANES:(p + 1) * LANES, tb * BLOCK:(tb + 1) * BLOCK]


def _proj_kernel(*refs, plain_transposed, with_z):
    if with_z:
        h_ref, wn_ref, wt_ref, wp_ref, gn_ref, gt_ref, wz_ref, n_ref, t_ref, p_ref, z_ref = refs
    else:
        h_ref, wn_ref, wt_ref, wp_ref, gn_ref, gt_ref, n_ref, t_ref, p_ref = refs
    h = h_ref[...]
    tm = h.shape[0]
    tn_dims = (((0,), (1,)), ((), ()))
    lane = lax.broadcasted_iota(jnp.int32, (tm, LANES), 1)
    low = lane < HEAD_DIM

    xn = jnp.dot(h, wn_ref[...].astype(BF16), preferred_element_type=F32)
    for p in range(N_PAIRS):
        xq = xn[:, p * LANES:(p + 1) * LANES]
        x2 = xq * xq
        s_lo = jnp.sum(jnp.where(low, x2, 0.0), axis=-1, keepdims=True)
        s_hi = jnp.sum(jnp.where(low, 0.0, x2), axis=-1, keepdims=True)
        inv = lax.rsqrt(jnp.where(low, s_lo, s_hi) * (1.0 / HEAD_DIM) + NORM_EPS)
        n_ref[p] = (xq * inv * gn_ref[...]).astype(BF16)

    xt = lax.dot_general(wt_ref[...].astype(BF16), h, tn_dims, preferred_element_type=F32)
    for tb in range(tm // BLOCK):
        blk = xt[:, tb * BLOCK:(tb + 1) * BLOCK].reshape(N_HEADS, HEAD_DIM, BLOCK)
        ss = jnp.sum(blk * blk, axis=1, keepdims=True)
        tn = blk * lax.rsqrt(ss * (1.0 / HEAD_DIM) + NORM_EPS) * gt_ref[...]
        tn = tn.reshape(N_PAIRS, 2 * HEAD_DIM, BLOCK).astype(BF16)
        for p in range(N_PAIRS):
            t_ref[p, tb] = tn[p]

    wp = wp_ref[...].astype(BF16)
    if plain_transposed:
        _store_cols(p_ref, lax.dot_general(wp, h, tn_dims, preferred_element_type=F32).astype(BF16))
    else:
        _store_rows(p_ref, jnp.dot(h, wp, preferred_element_type=F32).astype(BF16))
    if with_z:
        _store_rows(z_ref, jnp.dot(h, wz_ref[...].astype(BF16), preferred_element_type=F32).astype(BF16))


def _project_group(h, g, w_all, cols, gn, gt, plain_transposed):
    nb, s, d = h.shape
    tm = PROJ_TILE
    steps = s // tm
    const = lambda shape: pl.BlockSpec(shape, lambda b, t: (0,) * len(shape))
    w_spec = lambda col: pl.BlockSpec((d, d), lambda b, t: (0, col), pipeline_mode=pl.Buffered(1))
    with_z = len(cols) == 4
    in_specs = [pl.BlockSpec((None, tm, d), lambda b, t: (b, t, 0)), w_spec(cols[0]), w_spec(cols[1]),
                w_spec(cols[2]), const((1, LANES)), const((N_HEADS, HEAD_DIM, BLOCK))]
    args = [h, w_all, w_all, w_all, gn, gt]
    pm_spec = pl.BlockSpec((None, N_PAIRS, tm, LANES), lambda b, t: (b, 0, t, 0))
    pm_shape = jax.ShapeDtypeStruct((nb, N_PAIRS, s, LANES), BF16)
    fm_spec = pl.BlockSpec((None, N_PAIRS, tm // BLOCK, LANES, BLOCK), lambda b, t: (b, 0, t, 0, 0))
    fm_shape = jax.ShapeDtypeStruct((nb, N_PAIRS, s // BLOCK, LANES, BLOCK), BF16)
    out_specs = [pm_spec, fm_spec, fm_spec if plain_transposed else pm_spec]
    out_shape = [pm_shape, fm_shape, fm_shape if plain_transposed else pm_shape]
    if with_z:
        in_specs.append(w_spec(cols[3]))
        args.append(w_all)
        out_specs.append(pm_spec)
        out_shape.append(pm_shape)
    return pl.pallas_call(
        functools.partial(_proj_kernel, plain_transposed=plain_transposed, with_z=with_z),
        grid=(nb, steps),
        in_specs=in_specs,
        out_specs=out_specs,
        out_shape=out_shape,
        compiler_params=_cparams(("arbitrary", "arbitrary")),
        name=f"proj_group{g}",
    )(*args)


def _alibi_slopes():
    return np.exp2(-8.0 * np.arange(1, N_HEADS + 1, dtype=np.float64) / N_HEADS)


def _bias_tables(dilation):
    kj = np.arange(2 * BLOCK)[:, None]
    qi = np.arange(BLOCK)[None, :]
    steps = qi + BLOCK - kj
    n_steps = BLOCK
    valid = (steps >= 0) & (steps <= n_steps)
    valid_first = valid & (kj >= BLOCK)
    dist = (steps * dilation).astype(np.float64)
    slopes = _alibi_slopes()
    gen = np.empty((2, N_PAIRS, 2 * BLOCK, 2 * BLOCK), np.float32)
    for p in range(N_PAIRS):
        for half in range(2):
            sl = slopes[2 * p + half]
            cols = slice(half * BLOCK, (half + 1) * BLOCK)
            gen[0, p, :, cols] = np.where(valid, -sl * dist, NEG_INF)
            gen[1, p, :, cols] = np.where(valid_first, -sl * dist, NEG_INF)
    first = np.ascontiguousarray(gen[1, :, BLOCK:, :])
    return gen, first


def _attn_banded_kernel(*refs, mode, seq_blocks):
    if mode == "chain":
        q_ref, kt_ref, v_ref, ktp_ref, vp_ref, bias_ref, o_ref, st_ref = refs
    else:
        q_ref, kt_ref, v_ref, bias_ref, o_ref, st_ref = refs
    lane = lax.broadcasted_iota(jnp.int32, (BLOCK, LANES), 1)
    low = lane < HEAD_DIM
    ones_cols = jnp.ones((2 * BLOCK, LANES), BF16)
    is_start = (pl.program_id(1) == 0).astype(jnp.int32)
    for bi in range(Q_BLOCKS):
        rows = slice(bi * BLOCK, (bi + 1) * BLOCK)
        stats = jnp.ones((BLOCK, LANES), F32)
        for p in range(N_PAIRS):
            qb = q_ref[p, rows, :]
            zero = jnp.zeros_like(qb)
            q2 = jnp.concatenate([jnp.where(low, qb, zero), jnp.where(low, zero, qb)], axis=0)
            k_cur = kt_ref[p, bi]
            v_cur = v_ref[p, rows, :]
            if bi % seq_blocks > 0:
                k_prev, v_prev, bias = kt_ref[p, bi - 1], v_ref[p, (bi - 1) * BLOCK:bi * BLOCK, :], bias_ref[0, p]
            elif mode == "chain":
                k_prev, v_prev, bias = ktp_ref[p, 0], vp_ref[p], bias_ref[is_start, p]
            else:
                k_prev, v_prev, bias = k_cur, v_cur, bias_ref[1, p]
            kk = jnp.concatenate([k_prev, k_cur], axis=1)
            vv = jnp.concatenate([jnp.concatenate([v_prev, v_cur], axis=0), ones_cols], axis=1)
            s = jnp.dot(q2, kk, preferred_element_type=F32) + bias
            m = jnp.max(s, axis=-1, keepdims=True)
            pe = jnp.exp(s - m).astype(BF16)
            r = jnp.dot(pe, vv, preferred_element_type=F32)
            o_ref[p, rows, :] = jnp.where(low, r[0:BLOCK, 0:LANES], r[BLOCK:2 * BLOCK, 0:LANES]).astype(BF16)
            stats = jnp.where(lane == 2 * p, m[0:BLOCK], stats)
            stats = jnp.where(lane == 2 * p + 1, m[BLOCK:2 * BLOCK], stats)
            stats = jnp.where(lane == N_HEADS + 2 * p, r[0:BLOCK, LANES:2 * LANES], stats)
            stats = jnp.where(lane == N_HEADS + 2 * p + 1, r[BLOCK:2 * BLOCK, LANES:2 * LANES], stats)
        st_ref[rows, :] = stats


def _attn_single_kernel(k_ref, qt_ref, vt_ref, bias_ref, o_ref, st_ref, stt):
    zeros = jnp.zeros((HEAD_DIM, BLOCK), BF16)
    stt[...] = jnp.ones(stt.shape, F32)
    for bi in range(Q_BLOCKS):
        rows = slice(bi * BLOCK, (bi + 1) * BLOCK)
        for p in range(N_PAIRS):
            qt = qt_ref[p, bi]
            q2t = jnp.concatenate(
                [jnp.concatenate([qt[0:HEAD_DIM], zeros], axis=0),
                 jnp.concatenate([zeros, qt[HEAD_DIM:2 * HEAD_DIM]], axis=0)], axis=1)
            st = jnp.dot(k_ref[p, rows, :], q2t, preferred_element_type=F32) + bias_ref[p]
            m = jnp.max(st, axis=0, keepdims=True)
            pe = jnp.exp(st - m)
            den = jnp.sum(pe, axis=0, keepdims=True)
            ot = jnp.dot(vt_ref[p, bi], pe.astype(BF16), preferred_element_type=F32)
            o_t = jnp.concatenate([ot[0:HEAD_DIM, 0:BLOCK], ot[HEAD_DIM:2 * HEAD_DIM, BLOCK:2 * BLOCK]], axis=0)
            o_ref[p, rows, :] = o_t.T.astype(BF16)
            stt[2 * p:2 * p + 1, :] = m[:, 0:BLOCK]
            stt[2 * p + 1:2 * p + 2, :] = m[:, BLOCK:2 * BLOCK]
            stt[N_HEADS + 2 * p:N_HEADS + 2 * p + 1, :] = den[:, 0:BLOCK]
            stt[N_HEADS + 2 * p + 1:N_HEADS + 2 * p + 2, :] = den[:, BLOCK:2 * BLOCK]
        st_ref[rows, :] = stt[...].T


def _attention_group(g, dilation, tok_major, feat_major, plain):
    nb, _, s, _ = tok_major.shape
    length = s // dilation
    tm = ATTN_TILE
    gen_t, first_t = _bias_tables(dilation)
    const = lambda shape: pl.BlockSpec(shape, lambda b, t: (0,) * len(shape))
    pm_spec = pl.BlockSpec((None, N_PAIRS, tm, LANES), lambda b, t: (b, 0, t, 0))
    fm_spec = pl.BlockSpec((None, N_PAIRS, Q_BLOCKS, LANES, BLOCK), lambda b, t: (b, 0, t, 0, 0))
    scratch = []
    if length == BLOCK:
        body = _attn_single_kernel
        in_specs = [pm_spec, fm_spec, fm_spec, const(first_t.shape)]
        args = [tok_major, feat_major, plain, jnp.asarray(first_t)]
        scratch = [pltpu.VMEM((BLOCK, LANES), F32)]
    else:
        gen = np.ascontiguousarray(np.concatenate(
            [gen_t[..., 0:BLOCK].swapaxes(2, 3), gen_t[..., BLOCK:].swapaxes(2, 3)], axis=2))
        in_specs = [pm_spec, fm_spec, pm_spec]
        args = [tok_major, feat_major, plain]
        if length > tm:
            mode = "chain"
            prev = lambda t: jnp.maximum(t * Q_BLOCKS - 1, 0)
            in_specs += [
                pl.BlockSpec((None, N_PAIRS, 1, LANES, BLOCK), lambda b, t: (b, 0, prev(t), 0, 0)),
                pl.BlockSpec((None, N_PAIRS, BLOCK, LANES), lambda b, t: (b, 0, prev(t), 0)),
            ]
            args += [feat_major, plain]
        else:
            assert tm % length == 0
            mode = "class"
        in_specs.append(const(gen.shape))
        args.append(jnp.asarray(gen))
        body = functools.partial(_attn_banded_kernel, mode=mode, seq_blocks=min(length, tm) // BLOCK)
    return pl.pallas_call(
        body,
        grid=(nb, s // tm),
        in_specs=in_specs,
        out_specs=[pm_spec, pl.BlockSpec((None, tm, LANES), lambda b, t: (b, t, 0))],
        out_shape=[jax.ShapeDtypeStruct((nb, N_PAIRS, s, LANES), BF16), jax.ShapeDtypeStruct((nb, s, LANES), F32)],
        scratch_shapes=scratch,
        compiler_params=_cparams(("arbitrary", "arbitrary")),
        name=f"attn_group{g}",
    )(*args)


def _merge_kernel(o1_ref, o2_ref, o3_ref, s1_ref, s2_ref, s3_ref, z_ref, x_ref, mod_ref, e_ref, wout_ref,
                  unperm_ref, out_ref, stn2, stn3):
    o_refs = [o1_ref, o2_ref, o3_ref]
    st_refs = [s1_ref, s2_ref, s3_ref]
    stn_bufs = [None, stn2, stn3]
    stats, ogs = [], []
    for g in range(3):
        dil, per_class = st_refs[g].shape[0], st_refs[g].shape[1]
        if dil == 1:
            stats.append(st_refs[g][0])
            ogs.append(jnp.concatenate([o_refs[g][p, 0] for p in range(N_PAIRS)], axis=1).astype(F32))
            continue
        for r in range(dil):
            stn_bufs[g][pl.ds(r, per_class, stride=dil), :] = st_refs[g][r]
        stats.append(stn_bufs[g][...])
        span = UNPERM_ROWS // dil
        blocks = []
        for b in range(dil * per_class // UNPERM_ROWS):
            xb = jnp.concatenate(
                [jnp.concatenate([o_refs[g][p, r, b * span:(b + 1) * span, :] for r in range(dil)], axis=0)
                 for p in range(N_PAIRS)], axis=1)
            blocks.append(jnp.dot(unperm_ref[g - 1], xb, preferred_element_type=F32))
        ogs.append(jnp.concatenate(blocks, axis=0))
    dens = [pltpu.roll(st, LANES - N_HEADS, axis=1) for st in stats]
    m = jnp.maximum(jnp.maximum(stats[0], stats[1]), stats[2])
    es = [jnp.exp(st - m) for st in stats]
    den = es[0] * dens[0] + es[1] * dens[1] + es[2] * dens[2]
    inv = 1.0 / den
    lane = lax.broadcasted_iota(jnp.int32, m.shape, 1)
    acc = None
    for g in range(3):
        w = jnp.where(lane < N_HEADS, es[g] * inv, 0.0)
        w_hi = w.astype(BF16)
        w_lo = (w - w_hi.astype(F32)).astype(BF16)
        wx = jnp.dot(jnp.concatenate([w_hi, w_lo], axis=1), e_ref[...], preferred_element_type=F32)
        acc = wx * ogs[g] if acc is None else acc + wx * ogs[g]
    z = jnp.concatenate([z_ref[p] for p in range(N_PAIRS)], axis=1).astype(F32)
    a = (acc * _silu(z)).astype(BF16)
    y = jnp.dot(a, wout_ref[...], preferred_element_type=F32)
    out_ref[...] = x_ref[...] + mod_ref[2:3, :] * y


def _merge(os_, stats, z, x1, mod, w_out):
    nb, s, d = x1.shape
    tm = TOK_TILE
    expand = np.zeros((2 * LANES, d), np.float32)
    for hd in range(N_HEADS):
        expand[hd, hd * HEAD_DIM:(hd + 1) * HEAD_DIM] = 1.0
        expand[LANES + hd, hd * HEAD_DIM:(hd + 1) * HEAD_DIM] = 1.0
    pm_spec = pl.BlockSpec((None, N_PAIRS, tm, LANES), lambda b, t: (b, 0, t, 0))
    x_spec = pl.BlockSpec((None, tm, d), lambda b, t: (b, t, 0))
    const = lambda shape: pl.BlockSpec(shape, lambda b, t: (0,) * len(shape))
    o_views, o_specs, st_views, st_specs, unperm = [], [], [], [], []
    for (_, dil), o, st in zip(DILATED_GROUPS, os_, stats):
        o_views.append(o.reshape(nb, N_PAIRS, dil, s // dil, LANES))
        o_specs.append(pl.BlockSpec((None, N_PAIRS, dil, tm // dil, LANES), lambda b, t: (b, 0, 0, t, 0)))
        st_views.append(st.reshape(nb, dil, s // dil, LANES))
        st_specs.append(pl.BlockSpec((None, dil, tm // dil, LANES), lambda b, t: (b, 0, t, 0)))
        if dil > 1:
            span = UNPERM_ROWS // dil
            pt = np.zeros((UNPERM_ROWS, UNPERM_ROWS), np.float32)
            for r in range(dil):
                for i in range(span):
                    pt[i * dil + r, r * span + i] = 1.0
            unperm.append(pt)
    unperm = jnp.asarray(np.stack(unperm), BF16)
    return pl.pallas_call(
        _merge_kernel,
        grid=(nb, s // tm),
        in_specs=o_specs + st_specs + [
            pm_spec, x_spec, pl.BlockSpec((None, None, 3, d), lambda b, t: (1, b, 0, 0)),
            const((2 * LANES, d)), const((d, d)), const(unperm.shape)],
        out_specs=x_spec,
        out_shape=jax.ShapeDtypeStruct((nb, s, d), F32),
        scratch_shapes=[pltpu.VMEM((tm, LANES), F32), pltpu.VMEM((tm, LANES), F32)],
        compiler_params=_cparams(("arbitrary", "arbitrary")),
        name="merge_out_proj",
    )(*o_views, *st_views, z, x1, mod, jnp.asarray(expand, BF16), w_out.astype(BF16), unperm)


def kernel(x, c, norm_g, ada_w, ada_b, a_w_in, a_conv_w, a_conv_b, a_ln_g, a_ln_b, a_w_out, b_w_in, b_q_norm,
           b_k_norm, b_w_out):
    d = D_MODEL
    s = x.shape[1]
    mod = _ada_modulation(c, ada_w, ada_b)
    x1, *h1s = _layer0(x, mod, norm_g, a_w_in[0], a_conv_w[0], a_conv_b[0], a_ln_g[0], a_ln_b[0], a_w_out[0])
    h1s = [h.reshape(x.shape) for h in h1s]

    w_b = b_w_in[0]
    z_col = 3 * len(DILATED_GROUPS)
    scale = HEAD_DIM ** -0.5
    os_, stats, z = [], [], None
    for g, (_, dilation) in enumerate(DILATED_GROUPS):
        q_col, k_col, v_col = 3 * g, 3 * g + 1, 3 * g + 2
        gq, gk = b_q_norm[0, g] * scale, b_k_norm[0, g]
        row_gain = lambda gain: jnp.tile(gain, 2).reshape(1, LANES)
        col_gain = lambda gain: jnp.broadcast_to(gain[None, :, None], (N_HEADS, HEAD_DIM, BLOCK))
        if s // dilation == BLOCK:
            outs = _project_group(h1s[g], g, w_b, (k_col, q_col, v_col), row_gain(gk), col_gain(gq), True)
        else:
            cols = (q_col, k_col, v_col) + ((z_col,) if g == 0 else ())
            outs = _project_group(h1s[g], g, w_b, cols, row_gain(gq), col_gain(gk), False)
        if g == 0:
            *outs, z = outs
        o, st = _attention_group(g, dilation, *outs)
        os_.append(o)
        stats.append(st)
    return _merge(os_, stats, z, x1, mod, b_w_out[0])
```

```python
import functools

import jax
import jax.numpy as jnp
import numpy as np
from jax import lax
from jax.experimental import pallas as pl
from jax.experimental.pallas import tpu as pltpu

F32 = jnp.float32
BF16 = jnp.bfloat16

D_MODEL = 1024
CONV_WIDTH = 31
HEAD_DIM = 64
N_HEADS = 16
N_PAIRS = N_HEADS // 2
DILATED_GROUPS = ((128, 1), (512, 4), (2048, 16))
BLOCK = 128
NORM_EPS = 1e-6
NEG_INF = -1e30
LANES = 128
SUBLANES = 8
VMEM_LIMIT = 56 * 1024 * 1024

TOK_TILE = 512
UNPERM_ROWS = 256
PROJ_TILE = 1024
ATTN_TILE = 1024
Q_BLOCKS = ATTN_TILE // BLOCK
L0_TILE = 512
HALO = 32
CONV_ROWS = 128


def _sigmoid(x):
    return jax.nn.sigmoid(x)


def _silu(x):
    return x * _sigmoid(x)


def _cparams(sem):
    return pltpu.CompilerParams(dimension_semantics=sem, vmem_limit_bytes=VMEM_LIMIT)


def _ada_kernel(c_ref, w_ref, b_ref, o_ref):
    c = c_ref[...]
    sc = _silu(c).astype(BF16)
    o_ref[...] = jnp.dot(sc, w_ref[...].astype(BF16), preferred_element_type=F32) + b_ref[...]


def _ada_modulation(c, ada_w, ada_b):
    depth, d, d3 = ada_w.shape
    nb = c.shape[0]
    cols = d3 // 2
    out = pl.pallas_call(
        _ada_kernel,
        grid=(depth, d3 // cols),
        in_specs=[
            pl.BlockSpec((nb, d), lambda l, j: (0, 0)),
            pl.BlockSpec((None, d, cols), lambda l, j: (l, 0, j)),
            pl.BlockSpec((None, 1, cols), lambda l, j: (l, 0, j)),
        ],
        out_specs=pl.BlockSpec((None, nb, cols), lambda l, j: (l, 0, j)),
        out_shape=jax.ShapeDtypeStruct((depth, nb, d3), F32),
        compiler_params=_cparams(("arbitrary", "arbitrary")),
        name="ada_modulation",
    )(c, ada_w, ada_b.reshape(depth, 1, d3))
    return out.reshape(depth, nb, 3, d)


def _layer0_kernel(x_ref, mod0_ref, mod1_ref, g0_ref, g1_ref, win_ref, cw_ref, cb_ref, lg_ref, lb_ref,
                   wout_ref, perm_ref, x1_ref, h1_ref, h1c4_ref, h1c16_ref, ubuf, sbuf, cbuf):
    ts = x_ref.shape[0]
    d = x_ref.shape[1]

    @pl.when(pl.program_id(1) == 0)
    def _():
        ubuf[0:HALO, :] = jnp.zeros((HALO, d), F32)

    shift, scale, gate = mod0_ref[0:1, :], mod0_ref[1:2, :], mod0_ref[2:3, :]
    gain0 = g0_ref[...] * (1.0 + scale)
    off = HALO - (CONV_WIDTH - 1)
    shifted_rows = sbuf.shape[1]
    x = x_ref[...]
    ms = jnp.mean(x * x, axis=-1, keepdims=True)
    h = x * lax.rsqrt(ms + NORM_EPS) * gain0 + shift
    proj = jnp.dot(h.astype(BF16), win_ref[...], preferred_element_type=F32)
    u = proj[:, 0:d] * _sigmoid(proj[:, d:2 * d])
    sz = _silu(proj[:, 2 * d:3 * d])
    ubuf[HALO:HALO + ts, :] = u

    for s in range(1, SUBLANES):
        sbuf[s - 1] = ubuf[s:s + shifted_rows, :]
    for ls in range(d // LANES):
        lsl = slice(ls * LANES, (ls + 1) * LANES)
        for rc in range(ts // CONV_ROWS):
            r0 = rc * CONV_ROWS
            acc = jnp.broadcast_to(cb_ref[0:1, lsl], (CONV_ROWS, LANES))
            for j in range(CONV_WIDTH):
                s = (off + j) % SUBLANES
                a = r0 + off + j - s
                if s == 0:
                    tap = ubuf[a:a + CONV_ROWS, lsl]
                else:
                    tap = sbuf[s - 1, a:a + CONV_ROWS, lsl]
                acc = acc + cw_ref[j:j + 1, lsl] * tap
            cbuf[r0:r0 + CONV_ROWS, lsl] = acc
    ubuf[0:HALO, :] = ubuf[ts:ts + HALO, :]

    cv = cbuf[...]
    mu = jnp.mean(cv, axis=-1, keepdims=True)
    xc = cv - mu
    var = jnp.mean(xc * xc, axis=-1, keepdims=True)
    y = xc * lax.rsqrt(var + NORM_EPS) * lg_ref[...] + lb_ref[...]
    a = (_silu(y) * sz).astype(BF16)
    out = jnp.dot(a, wout_ref[...], preferred_element_type=F32)
    x1 = x + gate * out
    x1_ref[...] = x1

    shift1, scale1 = mod1_ref[0:1, :], mod1_ref[1:2, :]
    ms1 = jnp.mean(x1 * x1, axis=-1, keepdims=True)
    h1 = x1 * lax.rsqrt(ms1 + NORM_EPS) * (g1_ref[...] * (1.0 + scale1)) + shift1
    h1b = h1.astype(BF16)
    h1_ref[...] = h1b
    hp = jnp.dot(perm_ref[...], h1b, preferred_element_type=F32).astype(BF16)
    for k, ref in enumerate((h1c4_ref, h1c16_ref)):
        ref[...] = hp[k * ts:(k + 1) * ts].reshape(ref.shape)


def _layer0(x, mod, norm_g, w_in, conv_w, conv_b, ln_g, ln_b, w_out):
    nb, s, d = x.shape
    ts = L0_TILE
    dils = [dil for _, dil in DILATED_GROUPS if dil > 1]
    row = lambda a: a.reshape(1, d)
    const = lambda shape: pl.BlockSpec(shape, lambda b, t: (0,) * len(shape))
    once = lambda shape: pl.BlockSpec(shape, lambda b, t: (0,) * len(shape), pipeline_mode=pl.Buffered(1))
    perm = np.zeros((len(dils) * ts, ts), np.float32)
    for k, dil in enumerate(dils):
        for r in range(dil):
            for i in range(ts // dil):
                perm[k * ts + r * (ts // dil) + i, i * dil + r] = 1.0
    return pl.pallas_call(
        _layer0_kernel,
        grid=(nb, s // ts),
        in_specs=[
            pl.BlockSpec((None, ts, d), lambda b, t: (b, t, 0)),
            pl.BlockSpec((None, None, 3, d), lambda b, t: (0, b, 0, 0)),
            pl.BlockSpec((None, None, 3, d), lambda b, t: (1, b, 0, 0)),
            const((1, d)), const((1, d)),
            once((d, 3 * d)),
            const((CONV_WIDTH, d)), const((1, d)), const((1, d)), const((1, d)),
            once((d, d)), once(perm.shape),
        ],
        out_specs=[
            pl.BlockSpec((None, ts, d), lambda b, t: (b, t, 0)),
            pl.BlockSpec((None, ts, d), lambda b, t: (b, t, 0)),
        ] + [pl.BlockSpec((None, dil, ts // dil, d), lambda b, t: (b, 0, t, 0)) for dil in dils],
        out_shape=[jax.ShapeDtypeStruct((nb, s, d), F32), jax.ShapeDtypeStruct((nb, s, d), BF16)]
        + [jax.ShapeDtypeStruct((nb, dil, s // dil, d), BF16) for dil in dils],
        scratch_shapes=[pltpu.VMEM((HALO + ts, d), F32),
                        pltpu.VMEM((SUBLANES - 1, HALO + ts - SUBLANES, d), F32),
                        pltpu.VMEM((ts, d), F32)],
        compiler_params=_cparams(("arbitrary", "arbitrary")),
        name="layer0_conv_mixer",
    )(x, mod, mod, row(norm_g[0]), row(norm_g[1]), w_in.astype(BF16), conv_w, row(conv_b), row(ln_g),
      row(ln_b), w_out.astype(BF16), jnp.asarray(perm, BF16))


def _store_rows(ref, x):
    for p in range(N_PAIRS):
        ref[p] = x[:, p * LANES:(p + 1) * LANES]


def _store_cols(ref, xt):
    for tb in range(xt.shape[1] // BLOCK):
        for p in range(N_PAIRS):
            ref[p, tb] = xt[p * LANES:(p + 1) * LANES, tb * BLOCK:(tb + 1) * BLOCK]


def _proj_kernel(h_ref, wn_ref, wt_ref, wp_ref, gn_ref, gt_ref, n_ref, t_ref, p_ref, *, plain_transposed):
    h = h_ref[...]
    tm = h.shape[0]
    tn_dims = (((0,), (1,)), ((), ()))
    lane = lax.broadcasted_iota(jnp.int32, (tm, LANES), 1)
    low = lane < HEAD_DIM

    xn = jnp.dot(h, wn_ref[...].astype(BF16), preferred_element_type=F32)
    for p in range(N_PAIRS):
        xq = xn[:, p * LANES:(p + 1) * LANES]
        x2 = xq * xq
        s_lo = jnp.sum(jnp.where(low, x2, 0.0), axis=-1, keepdims=True)
        s_hi = jnp.sum(jnp.where(low, 0.0, x2), axis=-1, keepdims=True)
        inv = lax.rsqrt(jnp.where(low, s_lo, s_hi) * (1.0 / HEAD_DIM) + NORM_EPS)
        n_ref[p] = (xq * inv * gn_ref[...]).astype(BF16)

    xt = lax.dot_general(wt_ref[...].astype(BF16), h, tn_dims, preferred_element_type=F32)
    for tb in range(tm // BLOCK):
        blk = xt[:, tb * BLOCK:(tb + 1) * BLOCK].reshape(N_HEADS, HEAD_DIM, BLOCK)
        ss = jnp.sum(blk * blk, axis=1, keepdims=True)
        tn = blk * lax.rsqrt(ss * (1.0 / HEAD_DIM) + NORM_EPS) * gt_ref[...]
        tn = tn.reshape(N_PAIRS, 2 * HEAD_DIM, BLOCK).astype(BF16)
        for p in range(N_PAIRS):
            t_ref[p, tb] = tn[p]

    wp = wp_ref[...].astype(BF16)
    if plain_transposed:
        _store_cols(p_ref, lax.dot_general(wp, h, tn_dims, preferred_element_type=F32).astype(BF16))
    else:
        _store_rows(p_ref, jnp.dot(h, wp, preferred_element_type=F32).astype(BF16))


def _project_group(h, g, w_all, cols, gn, gt, plain_transposed):
    nb, s, d = h.shape
    tm = PROJ_TILE
    steps = s // tm
    const = lambda shape: pl.BlockSpec(shape, lambda b, t: (0,) * len(shape))
    w_spec = lambda col: pl.BlockSpec((d, d), lambda b, t: (0, col), pipeline_mode=pl.Buffered(1))
    in_specs = [pl.BlockSpec((None, tm, d), lambda b, t: (b, t, 0)), w_spec(cols[0]), w_spec(cols[1]),
                w_spec(cols[2]), const((1, LANES)), const((N_HEADS, HEAD_DIM, BLOCK))]
    args = [h, w_all, w_all, w_all, gn, gt]
    pm_spec = pl.BlockSpec((None, N_PAIRS, tm, LANES), lambda b, t: (b, 0, t, 0))
    pm_shape = jax.ShapeDtypeStruct((nb, N_PAIRS, s, LANES), BF16)
    fm_spec = pl.BlockSpec((None, N_PAIRS, tm // BLOCK, LANES, BLOCK), lambda b, t: (b, 0, t, 0, 0))
    fm_shape = jax.ShapeDtypeStruct((nb, N_PAIRS, s // BLOCK, LANES, BLOCK), BF16)
    out_specs = [pm_spec, fm_spec, fm_spec if plain_transposed else pm_spec]
    out_shape = [pm_shape, fm_shape, fm_shape if plain_transposed else pm_shape]
    return pl.pallas_call(
        functools.partial(_proj_kernel, plain_transposed=plain_transposed),
        grid=(nb, steps),
        in_specs=in_specs,
        out_specs=out_specs,
        out_shape=out_shape,
        compiler_params=_cparams(("arbitrary", "arbitrary")),
        name=f"proj_group{g}",
    )(*args)


def _alibi_slopes():
    return np.exp2(-8.0 * np.arange(1, N_HEADS + 1, dtype=np.float64) / N_HEADS)


def _bias_tables(dilation):
    kj = np.arange(2 * BLOCK)[:, None]
    qi = np.arange(BLOCK)[None, :]
    steps = qi + BLOCK - kj
    n_steps = BLOCK
    valid = (steps >= 0) & (steps <= n_steps)
    valid_first = valid & (kj >= BLOCK)
    dist = (steps * dilation).astype(np.float64)
    slopes = _alibi_slopes()
    gen = np.empty((2, N_PAIRS, 2 * BLOCK, 2 * BLOCK), np.float32)
    for p in range(N_PAIRS):
        for half in range(2):
            sl = slopes[2 * p + half]
            cols = slice(half * BLOCK, (half + 1) * BLOCK)
            gen[0, p, :, cols] = np.where(valid, -sl * dist, NEG_INF)
            gen[1, p, :, cols] = np.where(valid_first, -sl * dist, NEG_INF)
    first = np.ascontiguousarray(gen[1, :, BLOCK:, :])
    return gen, first


def _attn_banded_kernel(*refs, mode, seq_blocks):
    if mode == "chain":
        q_ref, kt_ref, v_ref, ktp_ref, vp_ref, bias_ref, o_ref, st_ref = refs
    else:
        q_ref, kt_ref, v_ref, bias_ref, o_ref, st_ref = refs
    lane = lax.broadcasted_iota(jnp.int32, (BLOCK, LANES), 1)
    low = lane < HEAD_DIM
    ones_cols = jnp.ones((2 * BLOCK, LANES), BF16)
    is_start = (pl.program_id(1) == 0).astype(jnp.int32)
    for bi in range(Q_BLOCKS):
        rows = slice(bi * BLOCK, (bi + 1) * BLOCK)
        stats = jnp.ones((BLOCK, LANES), F32)
        for p in range(N_PAIRS):
            qb = q_ref[p, rows, :]
            zero = jnp.zeros_like(qb)
            q2 = jnp.concatenate([jnp.where(low, qb, zero), jnp.where(low, zero, qb)], axis=0)
            k_cur = kt_ref[p, bi]
            v_cur = v_ref[p, rows, :]
            if bi % seq_blocks > 0:
                k_prev, v_prev, bias = kt_ref[p, bi - 1], v_ref[p, (bi - 1) * BLOCK:bi * BLOCK, :], bias_ref[0, p]
            elif mode == "chain":
                k_prev, v_prev, bias = ktp_ref[p, 0], vp_ref[p], bias_ref[is_start, p]
            else:
                k_prev, v_prev, bias = k_cur, v_cur, bias_ref[1, p]
            kk = jnp.concatenate([k_prev, k_cur], axis=1)
            vv = jnp.concatenate([jnp.concatenate([v_prev, v_cur], axis=0), ones_cols], axis=1)
            s = jnp.dot(q2, kk, preferred_element_type=F32) + bias
            m = jnp.max(s, axis=-1, keepdims=True)
            pe = jnp.exp(s - m).astype(BF16)
            r = jnp.dot(pe, vv, preferred_element_type=F32)
            o_ref[p, rows, :] = jnp.where(low, r[0:BLOCK, 0:LANES], r[BLOCK:2 * BLOCK, 0:LANES]).astype(BF16)
            stats = jnp.where(lane == 2 * p, m[0:BLOCK], stats)
            stats = jnp.where(lane == 2 * p + 1, m[BLOCK:2 * BLOCK], stats)
            stats = jnp.where(lane == N_HEADS + 2 * p, r[0:BLOCK, LANES:2 * LANES], stats)
            stats = jnp.where(lane == N_HEADS + 2 * p + 1, r[BLOCK:2 * BLOCK, LANES:2 * LANES], stats)
        st_ref[rows, :] = stats


def _attn_single_kernel(k_ref, qt_ref, vt_ref, bias_ref, o_ref, st_ref, stt):
    zeros = jnp.zeros((HEAD_DIM, BLOCK), BF16)
    stt[...] = jnp.ones(stt.shape, F32)
    for bi in range(Q_BLOCKS):
        rows = slice(bi * BLOCK, (bi + 1) * BLOCK)
        for p in range(N_PAIRS):
            qt = qt_ref[p, bi]
            q2t = jnp.concatenate(
                [jnp.concatenate([qt[0:HEAD_DIM], zeros], axis=0),
                 jnp.concatenate([zeros, qt[HEAD_DIM:2 * HEAD_DIM]], axis=0)], axis=1)
            st = jnp.dot(k_ref[p, rows, :], q2t, preferred_element_type=F32) + bias_ref[p]
            m = jnp.max(st, axis=0, keepdims=True)
            pe = jnp.exp(st - m)
            den = jnp.sum(pe, axis=0, keepdims=True)
            ot = jnp.dot(vt_ref[p, bi], pe.astype(BF16), preferred_element_type=F32)
            o_t = jnp.concatenate([ot[0:HEAD_DIM, 0:BLOCK], ot[HEAD_DIM:2 * HEAD_DIM, BLOCK:2 * BLOCK]], axis=0)
            o_ref[p, rows, :] = o_t.T.astype(BF16)
            stt[2 * p:2 * p + 1, :] = m[:, 0:BLOCK]
            stt[2 * p + 1:2 * p + 2, :] = m[:, BLOCK:2 * BLOCK]
            stt[N_HEADS + 2 * p:N_HEADS + 2 * p + 1, :] = den[:, 0:BLOCK]
            stt[N_HEADS + 2 * p + 1:N_HEADS + 2 * p + 2, :] = den[:, BLOCK:2 * BLOCK]
        st_ref[rows, :] = stt[...].T


def _attention_group(g, dilation, tok_major, feat_major, plain):
    nb, _, s, _ = tok_major.shape
    length = s // dilation
    tm = ATTN_TILE
    gen_t, first_t = _bias_tables(dilation)
    const = lambda shape: pl.BlockSpec(shape, lambda b, t: (0,) * len(shape))
    pm_spec = pl.BlockSpec((None, N_PAIRS, tm, LANES), lambda b, t: (b, 0, t, 0))
    fm_spec = pl.BlockSpec((None, N_PAIRS, Q_BLOCKS, LANES, BLOCK), lambda b, t: (b, 0, t, 0, 0))
    scratch = []
    if length == BLOCK:
        body = _attn_single_kernel
        in_specs = [pm_spec, fm_spec, fm_spec, const(first_t.shape)]
        args = [tok_major, feat_major, plain, jnp.asarray(first_t)]
        scratch = [pltpu.VMEM((BLOCK, LANES), F32)]
    else:
        gen = np.ascontiguousarray(np.concatenate(
            [gen_t[..., 0:BLOCK].swapaxes(2, 3), gen_t[..., BLOCK:].swapaxes(2, 3)], axis=2))
        in_specs = [pm_spec, fm_spec, pm_spec]
        args = [tok_major, feat_major, plain]
        if length > tm:
            mode = "chain"
            prev = lambda t: jnp.maximum(t * Q_BLOCKS - 1, 0)
            in_specs += [
                pl.BlockSpec((None, N_PAIRS, 1, LANES, BLOCK), lambda b, t: (b, 0, prev(t), 0, 0)),
                pl.BlockSpec((None, N_PAIRS, BLOCK, LANES), lambda b, t: (b, 0, prev(t), 0)),
            ]
            args += [feat_major, plain]
        else:
            assert tm % length == 0
            mode = "class"
        in_specs.append(const(gen.shape))
        args.append(jnp.asarray(gen))
        body = functools.partial(_attn_banded_kernel, mode=mode, seq_blocks=min(length, tm) // BLOCK)
    return pl.pallas_call(
        body,
        grid=(nb, s // tm),
        in_specs=in_specs,
        out_specs=[pm_spec, pl.BlockSpec((None, tm, LANES), lambda b, t: (b, t, 0))],
        out_shape=[jax.ShapeDtypeStruct((nb, N_PAIRS, s, LANES), BF16), jax.ShapeDtypeStruct((nb, s, LANES), F32)],
        scratch_shapes=scratch,
        compiler_params=_cparams(("arbitrary", "arbitrary")),
        name=f"attn_group{g}",
    )(*args)


def _merge_kernel(o1_ref, o2_ref, o3_ref, s1_ref, s2_ref, s3_ref, h_ref, wz_ref, x_ref, mod_ref, e_ref,
                  wout_ref, unperm_ref, out_ref, stn2, stn3):
    o_refs = [o1_ref, o2_ref, o3_ref]
    st_refs = [s1_ref, s2_ref, s3_ref]
    stn_bufs = [None, stn2, stn3]
    stats, ogs = [], []
    for g in range(3):
        dil, per_class = st_refs[g].shape[0], st_refs[g].shape[1]
        if dil == 1:
            stats.append(st_refs[g][0])
            ogs.append(jnp.concatenate([o_refs[g][p, 0] for p in range(N_PAIRS)], axis=1).astype(F32))
            continue
        for r in range(dil):
            stn_bufs[g][pl.ds(r, per_class, stride=dil), :] = st_refs[g][r]
        stats.append(stn_bufs[g][...])
        span = UNPERM_ROWS // dil
        blocks = []
        for b in range(dil * per_class // UNPERM_ROWS):
            xb = jnp.concatenate(
                [jnp.concatenate([o_refs[g][p, r, b * span:(b + 1) * span, :] for r in range(dil)], axis=0)
                 for p in range(N_PAIRS)], axis=1)
            blocks.append(jnp.dot(unperm_ref[g - 1], xb, preferred_element_type=F32))
        ogs.append(jnp.concatenate(blocks, axis=0))
    dens = [pltpu.roll(st, LANES - N_HEADS, axis=1) for st in stats]
    m = jnp.maximum(jnp.maximum(stats[0], stats[1]), stats[2])
    es = [jnp.exp(st - m) for st in stats]
    den = es[0] * dens[0] + es[1] * dens[1] + es[2] * dens[2]
    inv = 1.0 / den
    lane = lax.broadcasted_iota(jnp.int32, m.shape, 1)
    acc = None
    for g in range(3):
        w = jnp.where(lane < N_HEADS, es[g] * inv, 0.0)
        w_hi = w.astype(BF16)
        w_lo = (w - w_hi.astype(F32)).astype(BF16)
        wx = jnp.dot(jnp.concatenate([w_hi, w_lo], axis=1), e_ref[...], preferred_element_type=F32)
        acc = wx * ogs[g] if acc is None else acc + wx * ogs[g]
    z = jnp.dot(h_ref[...], wz_ref[...].astype(BF16), preferred_element_type=F32)
    a = (acc * _silu(z)).astype(BF16)
    y = jnp.dot(a, wout_ref[...], preferred_element_type=F32)
    out_ref[...] = x_ref[...] + mod_ref[2:3, :] * y


def _merge(os_, stats, h1, w_all, z_col, x1, mod, w_out):
    nb, s, d = x1.shape
    tm = TOK_TILE
    expand = np.zeros((2 * LANES, d), np.float32)
    for hd in range(N_HEADS):
        expand[hd, hd * HEAD_DIM:(hd + 1) * HEAD_DIM] = 1.0
        expand[LANES + hd, hd * HEAD_DIM:(hd + 1) * HEAD_DIM] = 1.0
    x_spec = pl.BlockSpec((None, tm, d), lambda b, t: (b, t, 0))
    const = lambda shape: pl.BlockSpec(shape, lambda b, t: (0,) * len(shape))
    o_views, o_specs, st_views, st_specs, unperm = [], [], [], [], []
    for (_, dil), o, st in zip(DILATED_GROUPS, os_, stats):
        o_views.append(o.reshape(nb, N_PAIRS, dil, s // dil, LANES))
        o_specs.append(pl.BlockSpec((None, N_PAIRS, dil, tm // dil, LANES), lambda b, t: (b, 0, 0, t, 0)))
        st_views.append(st.reshape(nb, dil, s // dil, LANES))
        st_specs.append(pl.BlockSpec((None, dil, tm // dil, LANES), lambda b, t: (b, 0, t, 0)))
        if dil > 1:
            span = UNPERM_ROWS // dil
            pt = np.zeros((UNPERM_ROWS, UNPERM_ROWS), np.float32)
            for r in range(dil):
                for i in range(span):
                    pt[i * dil + r, r * span + i] = 1.0
            unperm.append(pt)
    unperm = jnp.asarray(np.stack(unperm), BF16)
    return pl.pallas_call(
        _merge_kernel,
        grid=(nb, s // tm),
        in_specs=o_specs + st_specs + [
            pl.BlockSpec((None, tm, d), lambda b, t: (b, t, 0)),
            pl.BlockSpec((d, d), lambda b, t: (0, z_col), pipeline_mode=pl.Buffered(1)),
            x_spec, pl.BlockSpec((None, None, 3, d), lambda b, t: (1, b, 0, 0)),
            const((2 * LANES, d)), const((d, d)), const(unperm.shape)],
        out_specs=x_spec,
        out_shape=jax.ShapeDtypeStruct((nb, s, d), F32),
        scratch_shapes=[pltpu.VMEM((tm, LANES), F32), pltpu.VMEM((tm, LANES), F32)],
        compiler_params=_cparams(("arbitrary", "arbitrary")),
        name="merge_out_proj",
    )(*o_views, *st_views, h1, w_all, x1, mod, jnp.asarray(expand, BF16), w_out.astype(BF16), unperm)


def kernel(x, c, norm_g, ada_w, ada_b, a_w_in, a_conv_w, a_conv_b, a_ln_g, a_ln_b, a_w_out, b_w_in, b_q_norm,
           b_k_norm, b_w_out):
    d = D_MODEL
    s = x.shape[1]
    mod = _ada_modulation(c, ada_w, ada_b)
    x1, *h1s = _layer0(x, mod, norm_g, a_w_in[0], a_conv_w[0], a_conv_b[0], a_ln_g[0], a_ln_b[0], a_w_out[0])
    h1s = [h.reshape(x.shape) for h in h1s]

    w_b = b_w_in[0]
    z_col = 3 * len(DILATED_GROUPS)
    scale = HEAD_DIM ** -0.5
    os_, stats = [], []
    for g, (_, dilation) in enumerate(DILATED_GROUPS):
        q_col, k_col, v_col = 3 * g, 3 * g + 1, 3 * g + 2
        gq, gk = b_q_norm[0, g] * scale, b_k_norm[0, g]
        row_gain = lambda gain: jnp.tile(gain, 2).reshape(1, LANES)
        col_gain = lambda gain: jnp.broadcast_to(gain[None, :, None], (N_HEADS, HEAD_DIM, BLOCK))
        if s // dilation == BLOCK:
            outs = _project_group(h1s[g], g, w_b, (k_col, q_col, v_col), row_gain(gk), col_gain(gq), True)
        else:
            outs = _project_group(h1s[g], g, w_b, (q_col, k_col, v_col), row_gain(gq), col_gain(gk), False)
        o, st = _attention_group(g, dilation, *outs)
        os_.append(o)
        stats.append(st)
    return _merge(os_, stats, h1s[0], w_b, z_col, x1, mod, b_w_out[0])
```

```python
import functools

import jax
import jax.numpy as jnp
import numpy as np
from jax import lax
from jax.experimental import pallas as pl
from jax.experimental.pallas import tpu as pltpu

F32 = jnp.float32
BF16 = jnp.bfloat16

D_MODEL = 1024
CONV_WIDTH = 31
HEAD_DIM = 64
N_HEADS = 16
N_PAIRS = N_HEADS // 2
DILATED_GROUPS = ((128, 1), (512, 4), (2048, 16))
BLOCK = 128
NORM_EPS = 1e-6
NEG_INF = -1e30
LANES = 128
SUBLANES = 8
VMEM_LIMIT = 56 * 1024 * 1024

TOK_TILE = 512
UNPERM_ROWS = 256
PROJ_TILE = 1024
ATTN_TILE = 1024
Q_BLOCKS = ATTN_TILE // BLOCK
L0_TILE = 512
HALO = 32
CONV_ROWS = 128


def _sigmoid(x):
    return jax.nn.sigmoid(x)


def _silu(x):
    return x * _sigmoid(x)


def _cparams(sem):
    return pltpu.CompilerParams(dimension_semantics=sem, vmem_limit_bytes=VMEM_LIMIT)


def _ada_kernel(c_ref, w_ref, b_ref, o_ref):
    c = c_ref[...]
    sc = _silu(c).astype(BF16)
    o_ref[...] = jnp.dot(sc, w_ref[...].astype(BF16), preferred_element_type=F32) + b_ref[...]


def _ada_modulation(c, ada_w, ada_b):
    depth, d, d3 = ada_w.shape
    nb = c.shape[0]
    cols = d3 // 2
    out = pl.pallas_call(
        _ada_kernel,
        grid=(depth, d3 // cols),
        in_specs=[
            pl.BlockSpec((nb, d), lambda l, j: (0, 0)),
            pl.BlockSpec((None, d, cols), lambda l, j: (l, 0, j)),
            pl.BlockSpec((None, 1, cols), lambda l, j: (l, 0, j)),
        ],
        out_specs=pl.BlockSpec((None, nb, cols), lambda l, j: (l, 0, j)),
        out_shape=jax.ShapeDtypeStruct((depth, nb, d3), F32),
        compiler_params=_cparams(("arbitrary", "arbitrary")),
        name="ada_modulation",
    )(c, ada_w, ada_b.reshape(depth, 1, d3))
    return out.reshape(depth, nb, 3, d)


def _layer0_kernel(x_ref, mod0_ref, mod1_ref, g0_ref, g1_ref, win_ref, cw_ref, cb_ref, lg_ref, lb_ref,
                   wout_ref, perm_ref, x1_ref, h1_ref, h1c4_ref, h1c16_ref, ubuf, sbuf, cbuf):
    ts = x_ref.shape[0]
    d = x_ref.shape[1]

    @pl.when(pl.program_id(1) == 0)
    def _():
        ubuf[0:HALO, :] = jnp.zeros((HALO, d), F32)

    shift, scale, gate = mod0_ref[0:1, :], mod0_ref[1:2, :], mod0_ref[2:3, :]
    gain0 = g0_ref[...] * (1.0 + scale)
    off = HALO - (CONV_WIDTH - 1)
    shifted_rows = sbuf.shape[1]
    x = x_ref[...]
    ms = jnp.mean(x * x, axis=-1, keepdims=True)
    h = x * lax.rsqrt(ms + NORM_EPS) * gain0 + shift
    proj = jnp.dot(h.astype(BF16), win_ref[...], preferred_element_type=F32)
    u = proj[:, 0:d] * _sigmoid(proj[:, d:2 * d])
    sz = _silu(proj[:, 2 * d:3 * d])
    ubuf[HALO:HALO + ts, :] = u

    for s in range(1, SUBLANES):
        sbuf[s - 1] = ubuf[s:s + shifted_rows, :]
    for ls in range(d // LANES):
        lsl = slice(ls * LANES, (ls + 1) * LANES)
        for rc in range(ts // CONV_ROWS):
            r0 = rc * CONV_ROWS
            acc = jnp.broadcast_to(cb_ref[0:1, lsl], (CONV_ROWS, LANES))
            for j in range(CONV_WIDTH):
                s = (off + j) % SUBLANES
                a = r0 + off + j - s
                if s == 0:
                    tap = ubuf[a:a + CONV_ROWS, lsl]
                else:
                    tap = sbuf[s - 1, a:a + CONV_ROWS, lsl]
                acc = acc + cw_ref[j:j + 1, lsl] * tap
            cbuf[r0:r0 + CONV_ROWS, lsl] = acc
    ubuf[0:HALO, :] = ubuf[ts:ts + HALO, :]

    cv = cbuf[...]
    mu = jnp.mean(cv, axis=-1, keepdims=True)
    xc = cv - mu
    var = jnp.mean(xc * xc, axis=-1, keepdims=True)
    y = xc * lax.rsqrt(var + NORM_EPS) * lg_ref[...] + lb_ref[...]
    a = (_silu(y) * sz).astype(BF16)
    out = jnp.dot(a, wout_ref[...], preferred_element_type=F32)
    x1 = x + gate * out
    x1_ref[...] = x1

    shift1, scale1 = mod1_ref[0:1, :], mod1_ref[1:2, :]
    ms1 = jnp.mean(x1 * x1, axis=-1, keepdims=True)
    h1 = x1 * lax.rsqrt(ms1 + NORM_EPS) * (g1_ref[...] * (1.0 + scale1)) + shift1
    h1b = h1.astype(BF16)
    h1_ref[...] = h1b
    hp = jnp.dot(perm_ref[...], h1b, preferred_element_type=F32).astype(BF16)
    for k, ref in enumerate((h1c4_ref, h1c16_ref)):
        ref[...] = hp[k * ts:(k + 1) * ts].reshape(ref.shape)


def _layer0(x, mod, norm_g, w_in, conv_w, conv_b, ln_g, ln_b, w_out):
    nb, s, d = x.shape
    ts = L0_TILE
    dils = [dil for _, dil in DILATED_GROUPS if dil > 1]
    row = lambda a: a.reshape(1, d)
    const = lambda shape: pl.BlockSpec(shape, lambda b, t: (0,) * len(shape))
    once = lambda shape: pl.BlockSpec(shape, lambda b, t: (0,) * len(shape), pipeline_mode=pl.Buffered(1))
    perm = np.zeros((len(dils) * ts, ts), np.float32)
    for k, dil in enumerate(dils):
        for r in range(dil):
            for i in range(ts // dil):
                perm[k * ts + r * (ts // dil) + i, i * dil + r] = 1.0
    return pl.pallas_call(
        _layer0_kernel,
        grid=(nb, s // ts),
        in_specs=[
            pl.BlockSpec((None, ts, d), lambda b, t: (b, t, 0)),
            pl.BlockSpec((None, None, 3, d), lambda b, t: (0, b, 0, 0)),
            pl.BlockSpec((None, None, 3, d), lambda b, t: (1, b, 0, 0)),
            const((1, d)), const((1, d)),
            once((d, 3 * d)),
            const((CONV_WIDTH, d)), const((1, d)), const((1, d)), const((1, d)),
            once((d, d)), once(perm.shape),
        ],
        out_specs=[
            pl.BlockSpec((None, ts, d), lambda b, t: (b, t, 0)),
            pl.BlockSpec((None, ts, d), lambda b, t: (b, t, 0)),
        ] + [pl.BlockSpec((None, dil, ts // dil, d), lambda b, t: (b, 0, t, 0)) for dil in dils],
        out_shape=[jax.ShapeDtypeStruct((nb, s, d), F32), jax.ShapeDtypeStruct((nb, s, d), BF16)]
        + [jax.ShapeDtypeStruct((nb, dil, s // dil, d), BF16) for dil in dils],
        scratch_shapes=[pltpu.VMEM((HALO + ts, d), F32),
                        pltpu.VMEM((SUBLANES - 1, HALO + ts - SUBLANES, d), F32),
                        pltpu.VMEM((ts, d), F32)],
        compiler_params=_cparams(("arbitrary", "arbitrary")),
        name="layer0_conv_mixer",
    )(x, mod, mod, row(norm_g[0]), row(norm_g[1]), w_in.astype(BF16), conv_w, row(conv_b), row(ln_g),
      row(ln_b), w_out.astype(BF16), jnp.asarray(perm, BF16))


def _store_rows(ref, x):
    for p in range(N_PAIRS):
        ref[p] = x[:, p * LANES:(p + 1) * LANES]


def _store_cols(ref, xt):
    for tb in range(xt.shape[1] // BLOCK):
        for p in range(N_PAIRS):
            ref[p, tb] = xt[p * LANES:(p + 1) * LANES, tb * BLOCK:(tb + 1) * BLOCK]


def _proj_kernel(h_ref, wn_ref, wt_ref, wp_ref, gn_ref, gt_ref, n_ref, t_ref, p_ref, *, plain_transposed):
    h = h_ref[...]
    tm = h.shape[0]
    tn_dims = (((0,), (1,)), ((), ()))
    lane = lax.broadcasted_iota(jnp.int32, (tm, LANES), 1)
    low = lane < HEAD_DIM

    xn = jnp.dot(h, wn_ref[...].astype(BF16), preferred_element_type=F32)
    for p in range(N_PAIRS):
        xq = xn[:, p * LANES:(p + 1) * LANES]
        x2 = xq * xq
        s_lo = jnp.sum(jnp.where(low, x2, 0.0), axis=-1, keepdims=True)
        s_hi = jnp.sum(jnp.where(low, 0.0, x2), axis=-1, keepdims=True)
        inv = lax.rsqrt(jnp.where(low, s_lo, s_hi) * (1.0 / HEAD_DIM) + NORM_EPS)
        n_ref[p] = (xq * inv * gn_ref[...]).astype(BF16)

    xt = lax.dot_general(wt_ref[...].astype(BF16), h, tn_dims, preferred_element_type=F32)
    for tb in range(tm // BLOCK):
        blk = xt[:, tb * BLOCK:(tb + 1) * BLOCK].reshape(N_HEADS, HEAD_DIM, BLOCK)
        ss = jnp.sum(blk * blk, axis=1, keepdims=True)
        tn = blk * lax.rsqrt(ss * (1.0 / HEAD_DIM) + NORM_EPS) * gt_ref[...]
        tn = tn.reshape(N_PAIRS, 2 * HEAD_DIM, BLOCK).astype(BF16)
        for p in range(N_PAIRS):
            t_ref[p, tb] = tn[p]

    wp = wp_ref[...].astype(BF16)
    if plain_transposed:
        _store_cols(p_ref, lax.dot_general(wp, h, tn_dims, preferred_element_type=F32).astype(BF16))
    else:
        _store_rows(p_ref, jnp.dot(h, wp, preferred_element_type=F32).astype(BF16))


def _project_group(h, g, w_all, cols, gn, gt, plain_transposed):
    nb, s, d = h.shape
    tm = PROJ_TILE
    steps = s // tm
    const = lambda shape: pl.BlockSpec(shape, lambda b, t: (0,) * len(shape))
    w_spec = lambda col: pl.BlockSpec((d, d), lambda b, t: (0, col), pipeline_mode=pl.Buffered(1))
    in_specs = [pl.BlockSpec((None, tm, d), lambda b, t: (b, t, 0)), w_spec(cols[0]), w_spec(cols[1]),
                w_spec(cols[2]), const((1, LANES)), const((N_HEADS, HEAD_DIM, BLOCK))]
    args = [h, w_all, w_all, w_all, gn, gt]
    pm_spec = pl.BlockSpec((None, N_PAIRS, tm, LANES), lambda b, t: (b, 0, t, 0))
    pm_shape = jax.ShapeDtypeStruct((nb, N_PAIRS, s, LANES), BF16)
    fm_spec = pl.BlockSpec((None, N_PAIRS, tm // BLOCK, LANES, BLOCK), lambda b, t: (b, 0, t, 0, 0))
    fm_shape = jax.ShapeDtypeStruct((nb, N_PAIRS, s // BLOCK, LANES, BLOCK), BF16)
    out_specs = [pm_spec, fm_spec, fm_spec if plain_transposed else pm_spec]
    out_shape = [pm_shape, fm_shape, fm_shape if plain_transposed else pm_shape]
    return pl.pallas_call(
        functools.partial(_proj_kernel, plain_transposed=plain_transposed),
        grid=(nb, steps),
        in_specs=in_specs,
        out_specs=out_specs,
        out_shape=out_shape,
        compiler_params=_cparams(("arbitrary", "arbitrary")),
        name=f"proj_group{g}",
    )(*args)


def _alibi_slopes():
    return np.exp2(-8.0 * np.arange(1, N_HEADS + 1, dtype=np.float64) / N_HEADS)


def _bias_tables(dilation):
    kj = np.arange(2 * BLOCK)[:, None]
    qi = np.arange(BLOCK)[None, :]
    steps = qi + BLOCK - kj
    n_steps = BLOCK
    valid = (steps >= 0) & (steps <= n_steps)
    valid_first = valid & (kj >= BLOCK)
    dist = (steps * dilation).astype(np.float64)
    slopes = _alibi_slopes()
    gen = np.empty((2, N_PAIRS, 2 * BLOCK, 2 * BLOCK), np.float32)
    for p in range(N_PAIRS):
        for half in range(2):
            sl = slopes[2 * p + half]
            cols = slice(half * BLOCK, (half + 1) * BLOCK)
            gen[0, p, :, cols] = np.where(valid, -sl * dist, NEG_INF)
            gen[1, p, :, cols] = np.where(valid_first, -sl * dist, NEG_INF)
    first = np.ascontiguousarray(gen[1, :, BLOCK:, :])
    return gen, first


def _attn_banded_kernel(*refs, mode, seq_blocks):
    if mode == "chain":
        q_ref, kt_ref, v_ref, ktp_ref, vp_ref, bias_ref, o_ref, st_ref = refs
    else:
        q_ref, kt_ref, v_ref, bias_ref, o_ref, st_ref = refs
    lane = lax.broadcasted_iota(jnp.int32, (BLOCK, LANES), 1)
    low = lane < HEAD_DIM
    ones_cols = jnp.ones((2 * BLOCK, LANES), BF16)
    is_start = (pl.program_id(1) == 0).astype(jnp.int32)
    for bi in range(Q_BLOCKS):
        rows = slice(bi * BLOCK, (bi + 1) * BLOCK)
        stats = jnp.ones((BLOCK, LANES), F32)
        for p in range(N_PAIRS):
            qb = q_ref[p, rows, :]
            zero = jnp.zeros_like(qb)
            q2 = jnp.concatenate([jnp.where(low, qb, zero), jnp.where(low, zero, qb)], axis=0)
            k_cur = kt_ref[p, bi]
            v_cur = v_ref[p, rows, :]
            if bi % seq_blocks > 0:
                k_prev, v_prev, bias = kt_ref[p, bi - 1], v_ref[p, (bi - 1) * BLOCK:bi * BLOCK, :], bias_ref[0, p]
            elif mode == "chain":
                k_prev, v_prev, bias = ktp_ref[p, 0], vp_ref[p], bias_ref[is_start, p]
            else:
                k_prev, v_prev, bias = k_cur, v_cur, bias_ref[1, p]
            kk = jnp.concatenate([k_prev, k_cur], axis=1)
            vv = jnp.concatenate([jnp.concatenate([v_prev, v_cur], axis=0), ones_cols], axis=1)
            s = jnp.dot(q2, kk, preferred_element_type=F32) + bias
            m = jnp.max(s, axis=-1, keepdims=True)
            pe = jnp.exp(s - m).astype(BF16)
            r = jnp.dot(pe, vv, preferred_element_type=F32)
            o_ref[p, rows, :] = jnp.where(low, r[0:BLOCK, 0:LANES], r[BLOCK:2 * BLOCK, 0:LANES]).astype(BF16)
            stats = jnp.where(lane == 2 * p, m[0:BLOCK], stats)
            stats = jnp.where(lane == 2 * p + 1, m[BLOCK:2 * BLOCK], stats)
            stats = jnp.where(lane == N_HEADS + 2 * p, r[0:BLOCK, LANES:2 * LANES], stats)
            stats = jnp.where(lane == N_HEADS + 2 * p + 1, r[BLOCK:2 * BLOCK, LANES:2 * LANES], stats)
        st_ref[rows, :] = stats


def _attn_single_kernel(k_ref, qt_ref, vt_ref, bias_ref, o_ref, st_ref, stt):
    zeros = jnp.zeros((HEAD_DIM, BLOCK), BF16)
    stt[...] = jnp.ones(stt.shape, F32)
    for bi in range(Q_BLOCKS):
        rows = slice(bi * BLOCK, (bi + 1) * BLOCK)
        for p in range(N_PAIRS):
            qt = qt_ref[p, bi]
            q2t = jnp.concatenate(
                [jnp.concatenate([qt[0:HEAD_DIM], zeros], axis=0),
                 jnp.concatenate([zeros, qt[HEAD_DIM:2 * HEAD_DIM]], axis=0)], axis=1)
            st = jnp.dot(k_ref[p, rows, :], q2t, preferred_element_type=F32) + bias_ref[p]
            m = jnp.max(st, axis=0, keepdims=True)
            pe = jnp.exp(st - m)
            den = jnp.sum(pe, axis=0, keepdims=True)
            ot = jnp.dot(vt_ref[p, bi], pe.astype(BF16), preferred_element_type=F32)
            o_t = jnp.concatenate([ot[0:HEAD_DIM, 0:BLOCK], ot[HEAD_DIM:2 * HEAD_DIM, BLOCK:2 * BLOCK]], axis=0)
            o_ref[p, rows, :] = o_t.T.astype(BF16)
            stt[2 * p:2 * p + 1, :] = m[:, 0:BLOCK]
            stt[2 * p + 1:2 * p + 2, :] = m[:, BLOCK:2 * BLOCK]
            stt[N_HEADS + 2 * p:N_HEADS + 2 * p + 1, :] = den[:, 0:BLOCK]
            stt[N_HEADS + 2 * p + 1:N_HEADS + 2 * p + 2, :] = den[:, BLOCK:2 * BLOCK]
        st_ref[rows, :] = stt[...].T


def _attention_group(g, dilation, tok_major, feat_major, plain):
    nb, _, s, _ = tok_major.shape
    length = s // dilation
    tm = ATTN_TILE
    gen_t, first_t = _bias_tables(dilation)
    const = lambda shape: pl.BlockSpec(shape, lambda b, t: (0,) * len(shape))
    pm_spec = pl.BlockSpec((None, N_PAIRS, tm, LANES), lambda b, t: (b, 0, t, 0))
    fm_spec = pl.BlockSpec((None, N_PAIRS, Q_BLOCKS, LANES, BLOCK), lambda b, t: (b, 0, t, 0, 0))
    scratch = []
    if length == BLOCK:
        body = _attn_single_kernel
        in_specs = [pm_spec, fm_spec, fm_spec, const(first_t.shape)]
        args = [tok_major, feat_major, plain, jnp.asarray(first_t)]
        scratch = [pltpu.VMEM((BLOCK, LANES), F32)]
    else:
        gen = np.ascontiguousarray(np.concatenate(
            [gen_t[..., 0:BLOCK].swapaxes(2, 3), gen_t[..., BLOCK:].swapaxes(2, 3)], axis=2))
        in_specs = [pm_spec, fm_spec, pm_spec]
        args = [tok_major, feat_major, plain]
        if length > tm:
            mode = "chain"
            prev = lambda t: jnp.maximum(t * Q_BLOCKS - 1, 0)
            in_specs += [
                pl.BlockSpec((None, N_PAIRS, 1, LANES, BLOCK), lambda b, t: (b, 0, prev(t), 0, 0)),
                pl.BlockSpec((None, N_PAIRS, BLOCK, LANES), lambda b, t: (b, 0, prev(t), 0)),
            ]
            args += [feat_major, plain]
        else:
            assert tm % length == 0
            mode = "class"
        in_specs.append(const(gen.shape))
        args.append(jnp.asarray(gen))
        body = functools.partial(_attn_banded_kernel, mode=mode, seq_blocks=min(length, tm) // BLOCK)
    return pl.pallas_call(
        body,
        grid=(nb, s // tm),
        in_specs=in_specs,
        out_specs=[pm_spec, pl.BlockSpec((None, tm, LANES), lambda b, t: (b, t, 0))],
        out_shape=[jax.ShapeDtypeStruct((nb, N_PAIRS, s, LANES), BF16), jax.ShapeDtypeStruct((nb, s, LANES), F32)],
        scratch_shapes=scratch,
        compiler_params=_cparams(("arbitrary", "arbitrary")),
        name=f"attn_group{g}",
    )(*args)


def _merge_kernel(o1_ref, o2_ref, o3_ref, s1_ref, s2_ref, s3_ref, h_ref, wz_ref, x_ref, mod_ref, wout_ref,
                  unperm_ref, out_ref, stn2, stn3):
    o_refs = [o1_ref, o2_ref, o3_ref]
    st_refs = [s1_ref, s2_ref, s3_ref]
    stn_bufs = [None, stn2, stn3]
    stats, ogs = [], []
    for g in range(3):
        dil, per_class = st_refs[g].shape[0], st_refs[g].shape[1]
        if dil == 1:
            stats.append(st_refs[g][0])
            ogs.append(jnp.concatenate([o_refs[g][p, 0] for p in range(N_PAIRS)], axis=1).astype(F32))
            continue
        for r in range(dil):
            stn_bufs[g][pl.ds(r, per_class, stride=dil), :] = st_refs[g][r]
        stats.append(stn_bufs[g][...])
        span = UNPERM_ROWS // dil
        blocks = []
        for b in range(dil * per_class // UNPERM_ROWS):
            xb = jnp.concatenate(
                [jnp.concatenate([o_refs[g][p, r, b * span:(b + 1) * span, :] for r in range(dil)], axis=0)
                 for p in range(N_PAIRS)], axis=1)
            blocks.append(jnp.dot(unperm_ref[g - 1], xb, preferred_element_type=F32))
        ogs.append(jnp.concatenate(blocks, axis=0))
    dens = [pltpu.roll(st, LANES - N_HEADS, axis=1) for st in stats]
    m = jnp.maximum(jnp.maximum(stats[0], stats[1]), stats[2])
    es = [jnp.exp(st - m) for st in stats]
    den = es[0] * dens[0] + es[1] * dens[1] + es[2] * dens[2]
    inv = 1.0 / den
    lane = lax.broadcasted_iota(jnp.int32, m.shape, 1)
    acc = None
    for g in range(3):
        w = es[g] * inv
        wx = jnp.concatenate(
            [jnp.take_along_axis(w, jnp.where(lane < HEAD_DIM, 2 * p, 2 * p + 1), axis=1)
             for p in range(N_PAIRS)], axis=1)
        acc = wx * ogs[g] if acc is None else acc + wx * ogs[g]
    z = jnp.dot(h_ref[...], wz_ref[...].astype(BF16), preferred_element_type=F32)
    a = (acc * _silu(z)).astype(BF16)
    y = jnp.dot(a, wout_ref[...], preferred_element_type=F32)
    out_ref[...] = x_ref[...] + mod_ref[2:3, :] * y


def _merge(os_, stats, h1, w_all, z_col, x1, mod, w_out):
    nb, s, d = x1.shape
    tm = TOK_TILE
    x_spec = pl.BlockSpec((None, tm, d), lambda b, t: (b, t, 0))
    const = lambda shape: pl.BlockSpec(shape, lambda b, t: (0,) * len(shape))
    o_views, o_specs, st_views, st_specs, unperm = [], [], [], [], []
    for (_, dil), o, st in zip(DILATED_GROUPS, os_, stats):
        o_views.append(o.reshape(nb, N_PAIRS, dil, s // dil, LANES))
        o_specs.append(pl.BlockSpec((None, N_PAIRS, dil, tm // dil, LANES), lambda b, t: (b, 0, 0, t, 0)))
        st_views.append(st.reshape(nb, dil, s // dil, LANES))
        st_specs.append(pl.BlockSpec((None, dil, tm // dil, LANES), lambda b, t: (b, 0, t, 0)))
        if dil > 1:
            span = UNPERM_ROWS // dil
            pt = np.zeros((UNPERM_ROWS, UNPERM_ROWS), np.float32)
            for r in range(dil):
                for i in range(span):
                    pt[i * dil + r, r * span + i] = 1.0
            unperm.append(pt)
    unperm = jnp.asarray(np.stack(unperm), BF16)
    return pl.pallas_call(
        _merge_kernel,
        grid=(nb, s // tm),
        in_specs=o_specs + st_specs + [
            pl.BlockSpec((None, tm, d), lambda b, t: (b, t, 0)),
            pl.BlockSpec((d, d), lambda b, t: (0, z_col), pipeline_mode=pl.Buffered(1)),
            x_spec, pl.BlockSpec((None, None, 3, d), lambda b, t: (1, b, 0, 0)),
            const((d, d)), const(unperm.shape)],
        out_specs=x_spec,
        out_shape=jax.ShapeDtypeStruct((nb, s, d), F32),
        scratch_shapes=[pltpu.VMEM((tm, LANES), F32), pltpu.VMEM((tm, LANES), F32)],
        compiler_params=_cparams(("arbitrary", "arbitrary")),
        name="merge_out_proj",
    )(*o_views, *st_views, h1, w_all, x1, mod, w_out.astype(BF16), unperm)


def kernel(x, c, norm_g, ada_w, ada_b, a_w_in, a_conv_w, a_conv_b, a_ln_g, a_ln_b, a_w_out, b_w_in, b_q_norm,
           b_k_norm, b_w_out):
    d = D_MODEL
    s = x.shape[1]
    mod = _ada_modulation(c, ada_w, ada_b)
    x1, *h1s = _layer0(x, mod, norm_g, a_w_in[0], a_conv_w[0], a_conv_b[0], a_ln_g[0], a_ln_b[0], a_w_out[0])
    h1s = [h.reshape(x.shape) for h in h1s]

    w_b = b_w_in[0]
    z_col = 3 * len(DILATED_GROUPS)
    scale = HEAD_DIM ** -0.5
    os_, stats = [], []
    for g, (_, dilation) in enumerate(DILATED_GROUPS):
        q_col, k_col, v_col = 3 * g, 3 * g + 1, 3 * g + 2
        gq, gk = b_q_norm[0, g] * scale, b_k_norm[0, g]
        row_gain = lambda gain: jnp.tile(gain, 2).reshape(1, LANES)
        col_gain = lambda gain: jnp.broadcast_to(gain[None, :, None], (N_HEADS, HEAD_DIM, BLOCK))
        if s // dilation == BLOCK:
            outs = _project_group(h1s[g], g, w_b, (k_col, q_col, v_col), row_gain(gk), col_gain(gq), True)
        else:
            outs = _project_group(h1s[g], g, w_b, (q_col, k_col, v_col), row_gain(gq), col_gain(gk), False)
        o, st = _attention_group(g, dilation, *outs)
        os_.append(o)
        stats.append(st)
    return _merge(os_, stats, h1s[0], w_b, z_col, x1, mod, b_w_out[0])
```

```python
import functools

import jax
import jax.numpy as jnp
import numpy as np
from jax import lax
from jax.experimental import pallas as pl
from jax.experimental.pallas import tpu as pltpu

F32 = jnp.float32
BF16 = jnp.bfloat16

D_MODEL = 1024
CONV_WIDTH = 31
HEAD_DIM = 64
N_HEADS = 16
N_PAIRS = N_HEADS // 2
DILATED_GROUPS = ((128, 1), (512, 4), (2048, 16))
BLOCK = 128
NORM_EPS = 1e-6
NEG_INF = -1e30
LANES = 128
SUBLANES = 8
VMEM_LIMIT = 56 * 1024 * 1024

TOK_TILE = 512
PERM_ROWS = 256
PROJ_TILE = 1024
ATTN_TILE = 1024
Q_BLOCKS = ATTN_TILE // BLOCK
L0_TILE = 512
HALO = 32
CONV_ROWS = 128


def _sigmoid(x):
    return jax.nn.sigmoid(x)


def _silu(x):
    return x * _sigmoid(x)


def _cparams(sem):
    return pltpu.CompilerParams(dimension_semantics=sem, vmem_limit_bytes=VMEM_LIMIT)


def _ada_kernel(c_ref, w_ref, b_ref, o_ref):
    c = c_ref[...]
    sc = _silu(c).astype(BF16)
    o_ref[...] = jnp.dot(sc, w_ref[...].astype(BF16), preferred_element_type=F32) + b_ref[...]


def _ada_modulation(c, ada_w, ada_b):
    depth, d, d3 = ada_w.shape
    nb = c.shape[0]
    cols = d3 // 2
    out = pl.pallas_call(
        _ada_kernel,
        grid=(depth, d3 // cols),
        in_specs=[
            pl.BlockSpec((nb, d), lambda l, j: (0, 0)),
            pl.BlockSpec((None, d, cols), lambda l, j: (l, 0, j)),
            pl.BlockSpec((None, 1, cols), lambda l, j: (l, 0, j)),
        ],
        out_specs=pl.BlockSpec((None, nb, cols), lambda l, j: (l, 0, j)),
        out_shape=jax.ShapeDtypeStruct((depth, nb, d3), F32),
        compiler_params=_cparams(("arbitrary", "arbitrary")),
        name="ada_modulation",
    )(c, ada_w, ada_b.reshape(depth, 1, d3))
    return out.reshape(depth, nb, 3, d)


def _layer0_kernel(x_ref, mod0_ref, mod1_ref, g0_ref, g1_ref, win_ref, cw_ref, cb_ref, lg_ref, lb_ref,
                   wout_ref, perm_ref, x1_ref, h1_ref, h1c4_ref, h1c16_ref, ubuf, sbuf, cbuf):
    ts = x_ref.shape[0]
    d = x_ref.shape[1]

    @pl.when(pl.program_id(1) == 0)
    def _():
        ubuf[0:HALO, :] = jnp.zeros((HALO, d), F32)

    shift, scale, gate = mod0_ref[0:1, :], mod0_ref[1:2, :], mod0_ref[2:3, :]
    gain0 = g0_ref[...] * (1.0 + scale)
    off = HALO - (CONV_WIDTH - 1)
    shifted_rows = sbuf.shape[1]
    x = x_ref[...]
    ms = jnp.mean(x * x, axis=-1, keepdims=True)
    h = x * lax.rsqrt(ms + NORM_EPS) * gain0 + shift
    proj = jnp.dot(h.astype(BF16), win_ref[...], preferred_element_type=F32)
    u = proj[:, 0:d] * _sigmoid(proj[:, d:2 * d])
    sz = _silu(proj[:, 2 * d:3 * d])
    ubuf[HALO:HALO + ts, :] = u

    for s in range(1, SUBLANES):
        sbuf[s - 1] = ubuf[s:s + shifted_rows, :]
    for ls in range(d // LANES):
        lsl = slice(ls * LANES, (ls + 1) * LANES)
        for rc in range(ts // CONV_ROWS):
            r0 = rc * CONV_ROWS
            acc = jnp.broadcast_to(cb_ref[0:1, lsl], (CONV_ROWS, LANES))
            for j in range(CONV_WIDTH):
                s = (off + j) % SUBLANES
                a = r0 + off + j - s
                if s == 0:
                    tap = ubuf[a:a + CONV_ROWS, lsl]
                else:
                    tap = sbuf[s - 1, a:a + CONV_ROWS, lsl]
                acc = acc + cw_ref[j:j + 1, lsl] * tap
            cbuf[r0:r0 + CONV_ROWS, lsl] = acc
    ubuf[0:HALO, :] = ubuf[ts:ts + HALO, :]

    cv = cbuf[...]
    mu = jnp.mean(cv, axis=-1, keepdims=True)
    xc = cv - mu
    var = jnp.mean(xc * xc, axis=-1, keepdims=True)
    y = xc * lax.rsqrt(var + NORM_EPS) * lg_ref[...] + lb_ref[...]
    a = (_silu(y) * sz).astype(BF16)
    out = jnp.dot(a, wout_ref[...], preferred_element_type=F32)
    x1 = x + gate * out
    x1_ref[...] = x1

    shift1, scale1 = mod1_ref[0:1, :], mod1_ref[1:2, :]
    ms1 = jnp.mean(x1 * x1, axis=-1, keepdims=True)
    h1 = x1 * lax.rsqrt(ms1 + NORM_EPS) * (g1_ref[...] * (1.0 + scale1)) + shift1
    h1b = h1.astype(BF16)
    h1_ref[...] = h1b
    for b in range(ts // PERM_ROWS):
        hp = jnp.dot(perm_ref[...], h1b[b * PERM_ROWS:(b + 1) * PERM_ROWS],
                     preferred_element_type=F32).astype(BF16)
        for k, ref in enumerate((h1c4_ref, h1c16_ref)):
            span = PERM_ROWS // ref.shape[0]
            ref[:, b * span:(b + 1) * span, :] = hp[k * PERM_ROWS:(k + 1) * PERM_ROWS].reshape(
                ref.shape[0], span, d)


def _layer0(x, mod, norm_g, w_in, conv_w, conv_b, ln_g, ln_b, w_out):
    nb, s, d = x.shape
    ts = L0_TILE
    dils = [dil for _, dil in DILATED_GROUPS if dil > 1]
    row = lambda a: a.reshape(1, d)
    const = lambda shape: pl.BlockSpec(shape, lambda b, t: (0,) * len(shape))
    once = lambda shape: pl.BlockSpec(shape, lambda b, t: (0,) * len(shape), pipeline_mode=pl.Buffered(1))
    perm = np.zeros((len(dils) * PERM_ROWS, PERM_ROWS), np.float32)
    for k, dil in enumerate(dils):
        for r in range(dil):
            for i in range(PERM_ROWS // dil):
                perm[k * PERM_ROWS + r * (PERM_ROWS // dil) + i, i * dil + r] = 1.0
    return pl.pallas_call(
        _layer0_kernel,
        grid=(nb, s // ts),
        in_specs=[
            pl.BlockSpec((None, ts, d), lambda b, t: (b, t, 0)),
            pl.BlockSpec((None, None, 3, d), lambda b, t: (0, b, 0, 0)),
            pl.BlockSpec((None, None, 3, d), lambda b, t: (1, b, 0, 0)),
            const((1, d)), const((1, d)),
            once((d, 3 * d)),
            const((CONV_WIDTH, d)), const((1, d)), const((1, d)), const((1, d)),
            once((d, d)), once(perm.shape),
        ],
        out_specs=[
            pl.BlockSpec((None, ts, d), lambda b, t: (b, t, 0)),
            pl.BlockSpec((None, ts, d), lambda b, t: (b, t, 0)),
        ] + [pl.BlockSpec((None, dil, ts // dil, d), lambda b, t: (b, 0, t, 0)) for dil in dils],
        out_shape=[jax.ShapeDtypeStruct((nb, s, d), F32), jax.ShapeDtypeStruct((nb, s, d), BF16)]
        + [jax.ShapeDtypeStruct((nb, dil, s // dil, d), BF16) for dil in dils],
        scratch_shapes=[pltpu.VMEM((HALO + ts, d), F32),
                        pltpu.VMEM((SUBLANES - 1, HALO + ts - SUBLANES, d), F32),
                        pltpu.VMEM((ts, d), F32)],
        compiler_params=_cparams(("arbitrary", "arbitrary")),
        name="layer0_conv_mixer",
    )(x, mod, mod, row(norm_g[0]), row(norm_g[1]), w_in.astype(BF16), conv_w, row(conv_b), row(ln_g),
      row(ln_b), w_out.astype(BF16), jnp.asarray(perm, BF16))


def _store_rows(ref, x):
    for p in range(N_PAIRS):
        ref[p] = x[:, p * LANES:(p + 1) * LANES]


def _store_cols(ref, xt):
    for tb in range(xt.shape[1] // BLOCK):
        for p in range(N_PAIRS):
            ref[p, tb] = xt[p * LANES:(p + 1) * LANES, tb * BLOCK:(tb + 1) * BLOCK]


def _proj_kernel(h_ref, wn_ref, wt_ref, wp_ref, gn_ref, gt_ref, n_ref, t_ref, p_ref, *, plain_transposed):
    h = h_ref[...]
    tm = h.shape[0]
    tn_dims = (((0,), (1,)), ((), ()))
    lane = lax.broadcasted_iota(jnp.int32, (tm, LANES), 1)
    low = lane < HEAD_DIM

    xn = jnp.dot(h, wn_ref[...].astype(BF16), preferred_element_type=F32)
    for p in range(N_PAIRS):
        xq = xn[:, p * LANES:(p + 1) * LANES]
        x2 = xq * xq
        s_lo = jnp.sum(jnp.where(low, x2, 0.0), axis=-1, keepdims=True)
        s_hi = jnp.sum(jnp.where(low, 0.0, x2), axis=-1, keepdims=True)
        inv = lax.rsqrt(jnp.where(low, s_lo, s_hi) * (1.0 / HEAD_DIM) + NORM_EPS)
        n_ref[p] = (xq * inv * gn_ref[...]).astype(BF16)

    xt = lax.dot_general(wt_ref[...].astype(BF16), h, tn_dims, preferred_element_type=F32)
    for tb in range(tm // BLOCK):
        blk = xt[:, tb * BLOCK:(tb + 1) * BLOCK].reshape(N_HEADS, HEAD_DIM, BLOCK)
        ss = jnp.sum(blk * blk, axis=1, keepdims=True)
        tn = blk * lax.rsqrt(ss * (1.0 / HEAD_DIM) + NORM_EPS) * gt_ref[...]
        tn = tn.reshape(N_PAIRS, 2 * HEAD_DIM, BLOCK).astype(BF16)
        for p in range(N_PAIRS):
            t_ref[p, tb] = tn[p]

    wp = wp_ref[...].astype(BF16)
    if plain_transposed:
        _store_cols(p_ref, lax.dot_general(wp, h, tn_dims, preferred_element_type=F32).astype(BF16))
    else:
        _store_rows(p_ref, jnp.dot(h, wp, preferred_element_type=F32).astype(BF16))


def _project_group(h, g, w_all, cols, gn, gt, plain_transposed):
    nb, s, d = h.shape
    tm = PROJ_TILE
    steps = s // tm
    const = lambda shape: pl.BlockSpec(shape, lambda b, t: (0,) * len(shape))
    w_spec = lambda col: pl.BlockSpec((d, d), lambda b, t: (0, col), pipeline_mode=pl.Buffered(1))
    in_specs = [pl.BlockSpec((None, tm, d), lambda b, t: (b, t, 0)), w_spec(cols[0]), w_spec(cols[1]),
                w_spec(cols[2]), const((1, LANES)), const((N_HEADS, HEAD_DIM, BLOCK))]
    args = [h, w_all, w_all, w_all, gn, gt]
    pm_spec = pl.BlockSpec((None, N_PAIRS, tm, LANES), lambda b, t: (b, 0, t, 0))
    pm_shape = jax.ShapeDtypeStruct((nb, N_PAIRS, s, LANES), BF16)
    fm_spec = pl.BlockSpec((None, N_PAIRS, tm // BLOCK, LANES, BLOCK), lambda b, t: (b, 0, t, 0, 0))
    fm_shape = jax.ShapeDtypeStruct((nb, N_PAIRS, s // BLOCK, LANES, BLOCK), BF16)
    out_specs = [pm_spec, fm_spec, fm_spec if plain_transposed else pm_spec]
    out_shape = [pm_shape, fm_shape, fm_shape if plain_transposed else pm_shape]
    return pl.pallas_call(
        functools.partial(_proj_kernel, plain_transposed=plain_transposed),
        grid=(nb, steps),
        in_specs=in_specs,
        out_specs=out_specs,
        out_shape=out_shape,
        compiler_params=_cparams(("arbitrary", "arbitrary")),
        name=f"proj_group{g}",
    )(*args)


def _alibi_slopes():
    return np.exp2(-8.0 * np.arange(1, N_HEADS + 1, dtype=np.float64) / N_HEADS)


def _bias_tables(dilation):
    kj = np.arange(2 * BLOCK)[:, None]
    qi = np.arange(BLOCK)[None, :]
    steps = qi + BLOCK - kj
    n_steps = BLOCK
    valid = (steps >= 0) & (steps <= n_steps)
    valid_first = valid & (kj >= BLOCK)
    dist = (steps * dilation).astype(np.float64)
    slopes = _alibi_slopes()
    gen = np.empty((2, N_PAIRS, 2 * BLOCK, 2 * BLOCK), np.float32)
    for p in range(N_PAIRS):
        for half in range(2):
            sl = slopes[2 * p + half]
            cols = slice(half * BLOCK, (half + 1) * BLOCK)
            gen[0, p, :, cols] = np.where(valid, -sl * dist, NEG_INF)
            gen[1, p, :, cols] = np.where(valid_first, -sl * dist, NEG_INF)
    first = np.ascontiguousarray(gen[1, :, BLOCK:, :])
    return gen, first


def _attn_banded_kernel(*refs, mode, seq_blocks):
    if mode == "chain":
        q_ref, kt_ref, v_ref, ktp_ref, vp_ref, bias_ref, o_ref, st_ref = refs
    else:
        q_ref, kt_ref, v_ref, bias_ref, o_ref, st_ref = refs
    lane = lax.broadcasted_iota(jnp.int32, (BLOCK, LANES), 1)
    low = lane < HEAD_DIM
    ones_cols = jnp.ones((2 * BLOCK, LANES), BF16)
    is_start = (pl.program_id(1) == 0).astype(jnp.int32)
    for bi in range(Q_BLOCKS):
        rows = slice(bi * BLOCK, (bi + 1) * BLOCK)
        stats = jnp.ones((BLOCK, LANES), F32)
        for p in range(N_PAIRS):
            qb = q_ref[p, rows, :]
            zero = jnp.zeros_like(qb)
            q2 = jnp.concatenate([jnp.where(low, qb, zero), jnp.where(low, zero, qb)], axis=0)
            k_cur = kt_ref[p, bi]
            v_cur = v_ref[p, rows, :]
            if bi % seq_blocks > 0:
                k_prev, v_prev, bias = kt_ref[p, bi - 1], v_ref[p, (bi - 1) * BLOCK:bi * BLOCK, :], bias_ref[0, p]
            elif mode == "chain":
                k_prev, v_prev, bias = ktp_ref[p, 0], vp_ref[p], bias_ref[is_start, p]
            else:
                k_prev, v_prev, bias = k_cur, v_cur, bias_ref[1, p]
            kk = jnp.concatenate([k_prev, k_cur], axis=1)
            vv = jnp.concatenate([jnp.concatenate([v_prev, v_cur], axis=0), ones_cols], axis=1)
            s = jnp.dot(q2, kk, preferred_element_type=F32) + bias
            m = jnp.max(s, axis=-1, keepdims=True)
            pe = jnp.exp(s - m).astype(BF16)
            r = jnp.dot(pe, vv, preferred_element_type=F32)
            o_ref[p, rows, :] = jnp.where(low, r[0:BLOCK, 0:LANES], r[BLOCK:2 * BLOCK, 0:LANES]).astype(BF16)
            stats = jnp.where(lane == 2 * p, m[0:BLOCK], stats)
            stats = jnp.where(lane == 2 * p + 1, m[BLOCK:2 * BLOCK], stats)
            stats = jnp.where(lane == N_HEADS + 2 * p, r[0:BLOCK, LANES:2 * LANES], stats)
            stats = jnp.where(lane == N_HEADS + 2 * p + 1, r[BLOCK:2 * BLOCK, LANES:2 * LANES], stats)
        st_ref[rows, :] = stats


def _attn_single_kernel(k_ref, qt_ref, vt_ref, bias_ref, o_ref, st_ref, stt):
    zeros = jnp.zeros((HEAD_DIM, BLOCK), BF16)
    stt[...] = jnp.ones(stt.shape, F32)
    for bi in range(Q_BLOCKS):
        rows = slice(bi * BLOCK, (bi + 1) * BLOCK)
        for p in range(N_PAIRS):
            qt = qt_ref[p, bi]
            q2t = jnp.concatenate(
                [jnp.concatenate([qt[0:HEAD_DIM], zeros], axis=0),
                 jnp.concatenate([zeros, qt[HEAD_DIM:2 * HEAD_DIM]], axis=0)], axis=1)
            st = jnp.dot(k_ref[p, rows, :], q2t, preferred_element_type=F32) + bias_ref[p]
            m = jnp.max(st, axis=0, keepdims=True)
            pe = jnp.exp(st - m)
            den = jnp.sum(pe, axis=0, keepdims=True)
            ot = jnp.dot(vt_ref[p, bi], pe.astype(BF16), preferred_element_type=F32)
            o_t = jnp.concatenate([ot[0:HEAD_DIM, 0:BLOCK], ot[HEAD_DIM:2 * HEAD_DIM, BLOCK:2 * BLOCK]], axis=0)
            o_ref[p, rows, :] = o_t.T.astype(BF16)
            stt[2 * p:2 * p + 1, :] = m[:, 0:BLOCK]
            stt[2 * p + 1:2 * p + 2, :] = m[:, BLOCK:2 * BLOCK]
            stt[N_HEADS + 2 * p:N_HEADS + 2 * p + 1, :] = den[:, 0:BLOCK]
            stt[N_HEADS + 2 * p + 1:N_HEADS + 2 * p + 2, :] = den[:, BLOCK:2 * BLOCK]
        st_ref[rows, :] = stt[...].T


def _attention_group(g, dilation, tok_major, feat_major, plain):
    nb, _, s, _ = tok_major.shape
    length = s // dilation
    tm = ATTN_TILE
    gen_t, first_t = _bias_tables(dilation)
    const = lambda shape: pl.BlockSpec(shape, lambda b, t: (0,) * len(shape))
    pm_spec = pl.BlockSpec((None, N_PAIRS, tm, LANES), lambda b, t: (b, 0, t, 0))
    fm_spec = pl.BlockSpec((None, N_PAIRS, Q_BLOCKS, LANES, BLOCK), lambda b, t: (b, 0, t, 0, 0))
    scratch = []
    if length == BLOCK:
        body = _attn_single_kernel
        in_specs = [pm_spec, fm_spec, fm_spec, const(first_t.shape)]
        args = [tok_major, feat_major, plain, jnp.asarray(first_t)]
        scratch = [pltpu.VMEM((BLOCK, LANES), F32)]
    else:
        gen = np.ascontiguousarray(np.concatenate(
            [gen_t[..., 0:BLOCK].swapaxes(2, 3), gen_t[..., BLOCK:].swapaxes(2, 3)], axis=2))
        in_specs = [pm_spec, fm_spec, pm_spec]
        args = [tok_major, feat_major, plain]
        if length > tm:
            mode = "chain"
            prev = lambda t: jnp.maximum(t * Q_BLOCKS - 1, 0)
            in_specs += [
                pl.BlockSpec((None, N_PAIRS, 1, LANES, BLOCK), lambda b, t: (b, 0, prev(t), 0, 0)),
                pl.BlockSpec((None, N_PAIRS, BLOCK, LANES), lambda b, t: (b, 0, prev(t), 0)),
            ]
            args += [feat_major, plain]
        else:
            assert tm % length == 0
            mode = "class"
        in_specs.append(const(gen.shape))
        args.append(jnp.asarray(gen))
        body = functools.partial(_attn_banded_kernel, mode=mode, seq_blocks=min(length, tm) // BLOCK)
    return pl.pallas_call(
        body,
        grid=(nb, s // tm),
        in_specs=in_specs,
        out_specs=[pm_spec, pl.BlockSpec((None, tm, LANES), lambda b, t: (b, t, 0))],
        out_shape=[jax.ShapeDtypeStruct((nb, N_PAIRS, s, LANES), BF16), jax.ShapeDtypeStruct((nb, s, LANES), F32)],
        scratch_shapes=scratch,
        compiler_params=_cparams(("arbitrary", "arbitrary")),
        name=f"attn_group{g}",
    )(*args)


def _merge_kernel(o1_ref, o2_ref, o3_ref, s1_ref, s2_ref, s3_ref, h_ref, wz_ref, x_ref, mod_ref, wout_ref,
                  unperm_ref, out_ref, stn2, stn3):
    o_refs = [o1_ref, o2_ref, o3_ref]
    st_refs = [s1_ref, s2_ref, s3_ref]
    stn_bufs = [None, stn2, stn3]
    stats, ogs = [], []
    for g in range(3):
        dil, per_class = st_refs[g].shape[0], st_refs[g].shape[1]
        if dil == 1:
            stats.append(st_refs[g][0])
            ogs.append(jnp.concatenate([o_refs[g][p, 0] for p in range(N_PAIRS)], axis=1).astype(F32))
            continue
        for r in range(dil):
            stn_bufs[g][pl.ds(r, per_class, stride=dil), :] = st_refs[g][r]
        stats.append(stn_bufs[g][...])
        span = PERM_ROWS // dil
        blocks = []
        for b in range(dil * per_class // PERM_ROWS):
            xb = jnp.concatenate(
                [jnp.concatenate([o_refs[g][p, r, b * span:(b + 1) * span, :] for r in range(dil)], axis=0)
                 for p in range(N_PAIRS)], axis=1)
            blocks.append(jnp.dot(unperm_ref[g - 1], xb, preferred_element_type=F32))
        ogs.append(jnp.concatenate(blocks, axis=0))
    dens = [pltpu.roll(st, LANES - N_HEADS, axis=1) for st in stats]
    m = jnp.maximum(jnp.maximum(stats[0], stats[1]), stats[2])
    es = [jnp.exp(st - m) for st in stats]
    den = es[0] * dens[0] + es[1] * dens[1] + es[2] * dens[2]
    inv = 1.0 / den
    lane = lax.broadcasted_iota(jnp.int32, m.shape, 1)
    acc = None
    for g in range(3):
        w = es[g] * inv
        wx = jnp.concatenate(
            [jnp.take_along_axis(w, jnp.where(lane < HEAD_DIM, 2 * p, 2 * p + 1), axis=1)
             for p in range(N_PAIRS)], axis=1)
        acc = wx * ogs[g] if acc is None else acc + wx * ogs[g]
    z = jnp.dot(h_ref[...], wz_ref[...].astype(BF16), preferred_element_type=F32)
    a = (acc * _silu(z)).astype(BF16)
    y = jnp.dot(a, wout_ref[...], preferred_element_type=F32)
    out_ref[...] = x_ref[...] + mod_ref[2:3, :] * y


def _merge(os_, stats, h1, w_all, z_col, x1, mod, w_out):
    nb, s, d = x1.shape
    tm = TOK_TILE
    x_spec = pl.BlockSpec((None, tm, d), lambda b, t: (b, t, 0))
    const = lambda shape: pl.BlockSpec(shape, lambda b, t: (0,) * len(shape))
    o_views, o_specs, st_views, st_specs, unperm = [], [], [], [], []
    for (_, dil), o, st in zip(DILATED_GROUPS, os_, stats):
        o_views.append(o.reshape(nb, N_PAIRS, dil, s // dil, LANES))
        o_specs.append(pl.BlockSpec((None, N_PAIRS, dil, tm // dil, LANES), lambda b, t: (b, 0, 0, t, 0)))
        st_views.append(st.reshape(nb, dil, s // dil, LANES))
        st_specs.append(pl.BlockSpec((None, dil, tm // dil, LANES), lambda b, t: (b, 0, t, 0)))
        if dil > 1:
            span = PERM_ROWS // dil
            pt = np.zeros((PERM_ROWS, PERM_ROWS), np.float32)
            for r in range(dil):
                for i in range(span):
                    pt[i * dil + r, r * span + i] = 1.0
            unperm.append(pt)
    unperm = jnp.asarray(np.stack(unperm), BF16)
    return pl.pallas_call(
        _merge_kernel,
        grid=(nb, s // tm),
        in_specs=o_specs + st_specs + [
            pl.BlockSpec((None, tm, d), lambda b, t: (b, t, 0)),
            pl.BlockSpec((d, d), lambda b, t: (0, z_col), pipeline_mode=pl.Buffered(1)),
            x_spec, pl.BlockSpec((None, None, 3, d), lambda b, t: (1, b, 0, 0)),
            const((d, d)), const(unperm.shape)],
        out_specs=x_spec,
        out_shape=jax.ShapeDtypeStruct((nb, s, d), F32),
        scratch_shapes=[pltpu.VMEM((tm, LANES), F32), pltpu.VMEM((tm, LANES), F32)],
        compiler_params=_cparams(("arbitrary", "arbitrary")),
        name="merge_out_proj",
    )(*o_views, *st_views, h1, w_all, x1, mod, w_out.astype(BF16), unperm)


def kernel(x, c, norm_g, ada_w, ada_b, a_w_in, a_conv_w, a_conv_b, a_ln_g, a_ln_b, a_w_out, b_w_in, b_q_norm,
           b_k_norm, b_w_out):
    d = D_MODEL
    s = x.shape[1]
    mod = _ada_modulation(c, ada_w, ada_b)
    x1, *h1s = _layer0(x, mod, norm_g, a_w_in[0], a_conv_w[0], a_conv_b[0], a_ln_g[0], a_ln_b[0], a_w_out[0])
    h1s = [h.reshape(x.shape) for h in h1s]

    w_b = b_w_in[0]
    z_col = 3 * len(DILATED_GROUPS)
    scale = HEAD_DIM ** -0.5
    os_, stats = [], []
    for g, (_, dilation) in enumerate(DILATED_GROUPS):
        q_col, k_col, v_col = 3 * g, 3 * g + 1, 3 * g + 2
        gq, gk = b_q_norm[0, g] * scale, b_k_norm[0, g]
        row_gain = lambda gain: jnp.tile(gain, 2).reshape(1, LANES)
        col_gain = lambda gain: jnp.broadcast_to(gain[None, :, None], (N_HEADS, HEAD_DIM, BLOCK))
        if s // dilation == BLOCK:
            outs = _project_group(h1s[g], g, w_b, (k_col, q_col, v_col), row_gain(gk), col_gain(gq), True)
        else:
            outs = _project_group(h1s[g], g, w_b, (q_col, k_col, v_col), row_gain(gq), col_gain(gk), False)
        o, st = _attention_group(g, dilation, *outs)
        os_.append(o)
        stats.append(st)
    return _merge(os_, stats, h1s[0], w_b, z_col, x1, mod, b_w_out[0])
```

```python
import functools

import jax
import jax.numpy as jnp
import numpy as np
from jax import lax
from jax.experimental import pallas as pl
from jax.experimental.pallas import tpu as pltpu

F32 = jnp.float32
BF16 = jnp.bfloat16

D_MODEL = 1024
CONV_WIDTH = 31
HEAD_DIM = 64
N_HEADS = 16
N_PAIRS = N_HEADS // 2
DILATED_GROUPS = ((128, 1), (512, 4), (2048, 16))
BLOCK = 128
NORM_EPS = 1e-6
NEG_INF = -1e30
LANES = 128
SUBLANES = 8
VMEM_LIMIT = 56 * 1024 * 1024

TOK_TILE = 512
PERM_ROWS = 256
PROJ_TILE = 1024
ATTN_TILE = 1024
Q_BLOCKS = ATTN_TILE // BLOCK
L0_TILE = 512
HALO = 32
CONV_ROWS = 128


def _sigmoid(x):
    return jax.nn.sigmoid(x)


def _silu(x):
    return x * _sigmoid(x)


def _cparams(sem):
    return pltpu.CompilerParams(dimension_semantics=sem, vmem_limit_bytes=VMEM_LIMIT)


def _ada_kernel(c_ref, w_ref, b_ref, o_ref):
    c = c_ref[...]
    sc = _silu(c).astype(BF16)
    o_ref[...] = jnp.dot(sc, w_ref[...].astype(BF16), preferred_element_type=F32) + b_ref[...]


def _ada_modulation(c, ada_w, ada_b):
    depth, d, d3 = ada_w.shape
    nb = c.shape[0]
    cols = d3 // 2
    out = pl.pallas_call(
        _ada_kernel,
        grid=(depth, d3 // cols),
        in_specs=[
            pl.BlockSpec((nb, d), lambda l, j: (0, 0)),
            pl.BlockSpec((None, d, cols), lambda l, j: (l, 0, j)),
            pl.BlockSpec((None, 1, cols), lambda l, j: (l, 0, j)),
        ],
        out_specs=pl.BlockSpec((None, nb, cols), lambda l, j: (l, 0, j)),
        out_shape=jax.ShapeDtypeStruct((depth, nb, d3), F32),
        compiler_params=_cparams(("arbitrary", "arbitrary")),
        name="ada_modulation",
    )(c, ada_w, ada_b.reshape(depth, 1, d3))
    return out.reshape(depth, nb, 3, d)


def _layer0_kernel(x_ref, mod0_ref, mod1_ref, g0_ref, g1_ref, win_ref, cw_ref, cb_ref, lg_ref, lb_ref,
                   wout_ref, perm_ref, x1_ref, h1_ref, h1c4_ref, h1c16_ref, ubuf, sbuf, cbuf, szbuf):
    ts = x_ref.shape[0]
    d = x_ref.shape[1]

    @pl.when(pl.program_id(1) == 0)
    def _():
        ubuf[0:HALO, :] = jnp.zeros((HALO, d), F32)

    shift, scale, gate = mod0_ref[0:1, :], mod0_ref[1:2, :], mod0_ref[2:3, :]
    gain0 = g0_ref[...] * (1.0 + scale)
    off = HALO - (CONV_WIDTH - 1)
    shifted_rows = sbuf.shape[1]
    x = x_ref[...]
    ms = jnp.mean(x * x, axis=-1, keepdims=True)
    h = x * lax.rsqrt(ms + NORM_EPS) * gain0 + shift
    proj = jnp.dot(h.astype(BF16), win_ref[...], preferred_element_type=F32)
    ubuf[HALO:HALO + ts, :] = proj[:, 0:d] * _sigmoid(proj[:, d:2 * d])
    szbuf[...] = _silu(proj[:, 2 * d:3 * d])

    for s in range(1, SUBLANES):
        sbuf[s - 1] = ubuf[s:s + shifted_rows, :]
    for ls in range(d // LANES):
        lsl = slice(ls * LANES, (ls + 1) * LANES)
        for rc in range(ts // CONV_ROWS):
            r0 = rc * CONV_ROWS
            acc = jnp.broadcast_to(cb_ref[0:1, lsl], (CONV_ROWS, LANES))
            for j in range(CONV_WIDTH):
                s = (off + j) % SUBLANES
                a = r0 + off + j - s
                if s == 0:
                    tap = ubuf[a:a + CONV_ROWS, lsl]
                else:
                    tap = sbuf[s - 1, a:a + CONV_ROWS, lsl]
                acc = acc + cw_ref[j:j + 1, lsl] * tap
            cbuf[r0:r0 + CONV_ROWS, lsl] = acc
    ubuf[0:HALO, :] = ubuf[ts:ts + HALO, :]

    shift1, scale1 = mod1_ref[0:1, :], mod1_ref[1:2, :]
    gain1 = g1_ref[...] * (1.0 + scale1)
    for b in range(ts // PERM_ROWS):
        rows = slice(b * PERM_ROWS, (b + 1) * PERM_ROWS)
        cv = cbuf[rows, :]
        mu = jnp.mean(cv, axis=-1, keepdims=True)
        xc = cv - mu
        var = jnp.mean(xc * xc, axis=-1, keepdims=True)
        y = xc * lax.rsqrt(var + NORM_EPS) * lg_ref[...] + lb_ref[...]
        a = (_silu(y) * szbuf[rows, :]).astype(BF16)
        out = jnp.dot(a, wout_ref[...], preferred_element_type=F32)
        x1 = x_ref[rows, :] + gate * out
        x1_ref[rows, :] = x1

        ms1 = jnp.mean(x1 * x1, axis=-1, keepdims=True)
        h1b = (x1 * lax.rsqrt(ms1 + NORM_EPS) * gain1 + shift1).astype(BF16)
        h1_ref[rows, :] = h1b
        hp = jnp.dot(perm_ref[...], h1b, preferred_element_type=F32).astype(BF16)
        for k, ref in enumerate((h1c4_ref, h1c16_ref)):
            span = PERM_ROWS // ref.shape[0]
            ref[:, b * span:(b + 1) * span, :] = hp[k * PERM_ROWS:(k + 1) * PERM_ROWS].reshape(
                ref.shape[0], span, d)


def _layer0(x, mod, norm_g, w_in, conv_w, conv_b, ln_g, ln_b, w_out):
    nb, s, d = x.shape
    ts = L0_TILE
    dils = [dil for _, dil in DILATED_GROUPS if dil > 1]
    row = lambda a: a.reshape(1, d)
    const = lambda shape: pl.BlockSpec(shape, lambda b, t: (0,) * len(shape))
    once = lambda shape: pl.BlockSpec(shape, lambda b, t: (0,) * len(shape), pipeline_mode=pl.Buffered(1))
    perm = np.zeros((len(dils) * PERM_ROWS, PERM_ROWS), np.float32)
    for k, dil in enumerate(dils):
        for r in range(dil):
            for i in range(PERM_ROWS // dil):
                perm[k * PERM_ROWS + r * (PERM_ROWS // dil) + i, i * dil + r] = 1.0
    return pl.pallas_call(
        _layer0_kernel,
        grid=(nb, s // ts),
        in_specs=[
            pl.BlockSpec((None, ts, d), lambda b, t: (b, t, 0)),
            pl.BlockSpec((None, None, 3, d), lambda b, t: (0, b, 0, 0)),
            pl.BlockSpec((None, None, 3, d), lambda b, t: (1, b, 0, 0)),
            const((1, d)), const((1, d)),
            once((d, 3 * d)),
            const((CONV_WIDTH, d)), const((1, d)), const((1, d)), const((1, d)),
            once((d, d)), once(perm.shape),
        ],
        out_specs=[
            pl.BlockSpec((None, ts, d), lambda b, t: (b, t, 0)),
            pl.BlockSpec((None, ts, d), lambda b, t: (b, t, 0)),
        ] + [pl.BlockSpec((None, dil, ts // dil, d), lambda b, t: (b, 0, t, 0)) for dil in dils],
        out_shape=[jax.ShapeDtypeStruct((nb, s, d), F32), jax.ShapeDtypeStruct((nb, s, d), BF16)]
        + [jax.ShapeDtypeStruct((nb, dil, s // dil, d), BF16) for dil in dils],
        scratch_shapes=[pltpu.VMEM((HALO + ts, d), F32),
                        pltpu.VMEM((SUBLANES - 1, HALO + ts - SUBLANES, d), F32),
                        pltpu.VMEM((ts, d), F32), pltpu.VMEM((ts, d), F32)],
        compiler_params=_cparams(("arbitrary", "arbitrary")),
        name="layer0_conv_mixer",
    )(x, mod, mod, row(norm_g[0]), row(norm_g[1]), w_in.astype(BF16), conv_w, row(conv_b), row(ln_g),
      row(ln_b), w_out.astype(BF16), jnp.asarray(perm, BF16))


def _store_rows(ref, x):
    for p in range(N_PAIRS):
        ref[p] = x[:, p * LANES:(p + 1) * LANES]


def _store_cols(ref, xt):
    for tb in range(xt.shape[1] // BLOCK):
        for p in range(N_PAIRS):
            ref[p, tb] = xt[p * LANES:(p + 1) * LANES, tb * BLOCK:(tb + 1) * BLOCK]


def _proj_kernel(h_ref, wn_ref, wt_ref, wp_ref, gn_ref, gt_ref, n_ref, t_ref, p_ref, *, plain_transposed):
    h = h_ref[...]
    tm = h.shape[0]
    tn_dims = (((0,), (1,)), ((), ()))
    lane = lax.broadcasted_iota(jnp.int32, (tm, LANES), 1)
    low = lane < HEAD_DIM

    xn = jnp.dot(h, wn_ref[...].astype(BF16), preferred_element_type=F32)
    for p in range(N_PAIRS):
        xq = xn[:, p * LANES:(p + 1) * LANES]
        x2 = xq * xq
        s_lo = jnp.sum(jnp.where(low, x2, 0.0), axis=-1, keepdims=True)
        s_hi = jnp.sum(jnp.where(low, 0.0, x2), axis=-1, keepdims=True)
        inv = lax.rsqrt(jnp.where(low, s_lo, s_hi) * (1.0 / HEAD_DIM) + NORM_EPS)
        n_ref[p] = (xq * inv * gn_ref[...]).astype(BF16)

    xt = lax.dot_general(wt_ref[...].astype(BF16), h, tn_dims, preferred_element_type=F32)
    for tb in range(tm // BLOCK):
        blk = xt[:, tb * BLOCK:(tb + 1) * BLOCK].reshape(N_HEADS, HEAD_DIM, BLOCK)
        ss = jnp.sum(blk * blk, axis=1, keepdims=True)
        tn = blk * lax.rsqrt(ss * (1.0 / HEAD_DIM) + NORM_EPS) * gt_ref[...]
        tn = tn.reshape(N_PAIRS, 2 * HEAD_DIM, BLOCK).astype(BF16)
        for p in range(N_PAIRS):
            t_ref[p, tb] = tn[p]

    wp = wp_ref[...].astype(BF16)
    if plain_transposed:
        _store_cols(p_ref, lax.dot_general(wp, h, tn_dims, preferred_element_type=F32).astype(BF16))
    else:
        _store_rows(p_ref, jnp.dot(h, wp, preferred_element_type=F32).astype(BF16))


def _project_group(h, g, w_all, cols, gn, gt, plain_transposed):
    nb, s, d = h.shape
    tm = PROJ_TILE
    steps = s // tm
    const = lambda shape: pl.BlockSpec(shape, lambda b, t: (0,) * len(shape))
    w_spec = lambda col: pl.BlockSpec((d, d), lambda b, t: (0, col), pipeline_mode=pl.Buffered(1))
    in_specs = [pl.BlockSpec((None, tm, d), lambda b, t: (b, t, 0)), w_spec(cols[0]), w_spec(cols[1]),
                w_spec(cols[2]), const((1, LANES)), const((N_HEADS, HEAD_DIM, BLOCK))]
    args = [h, w_all, w_all, w_all, gn, gt]
    pm_spec = pl.BlockSpec((None, N_PAIRS, tm, LANES), lambda b, t: (b, 0, t, 0))
    pm_shape = jax.ShapeDtypeStruct((nb, N_PAIRS, s, LANES), BF16)
    fm_spec = pl.BlockSpec((None, N_PAIRS, tm // BLOCK, LANES, BLOCK), lambda b, t: (b, 0, t, 0, 0))
    fm_shape = jax.ShapeDtypeStruct((nb, N_PAIRS, s // BLOCK, LANES, BLOCK), BF16)
    out_specs = [pm_spec, fm_spec, fm_spec if plain_transposed else pm_spec]
    out_shape = [pm_shape, fm_shape, fm_shape if plain_transposed else pm_shape]
    return pl.pallas_call(
        functools.partial(_proj_kernel, plain_transposed=plain_transposed),
        grid=(nb, steps),
        in_specs=in_specs,
        out_specs=out_specs,
        out_shape=out_shape,
        compiler_params=_cparams(("arbitrary", "arbitrary")),
        name=f"proj_group{g}",
    )(*args)


def _alibi_slopes():
    return np.exp2(-8.0 * np.arange(1, N_HEADS + 1, dtype=np.float64) / N_HEADS)


def _bias_tables(dilation):
    kj = np.arange(2 * BLOCK)[:, None]
    qi = np.arange(BLOCK)[None, :]
    steps = qi + BLOCK - kj
    n_steps = BLOCK
    valid = (steps >= 0) & (steps <= n_steps)
    valid_first = valid & (kj >= BLOCK)
    dist = (steps * dilation).astype(np.float64)
    slopes = _alibi_slopes()
    gen = np.empty((2, N_PAIRS, 2 * BLOCK, 2 * BLOCK), np.float32)
    for p in range(N_PAIRS):
        for half in range(2):
            sl = slopes[2 * p + half]
            cols = slice(half * BLOCK, (half + 1) * BLOCK)
            gen[0, p, :, cols] = np.where(valid, -sl * dist, NEG_INF)
            gen[1, p, :, cols] = np.where(valid_first, -sl * dist, NEG_INF)
    first = np.ascontiguousarray(gen[1, :, BLOCK:, :])
    return gen, first


def _attn_banded_kernel(*refs, mode, seq_blocks):
    if mode == "chain":
        q_ref, kt_ref, v_ref, ktp_ref, vp_ref, bias_ref, o_ref, st_ref = refs
    else:
        q_ref, kt_ref, v_ref, bias_ref, o_ref, st_ref = refs
    lane = lax.broadcasted_iota(jnp.int32, (BLOCK, LANES), 1)
    low = lane < HEAD_DIM
    ones_cols = jnp.ones((2 * BLOCK, LANES), BF16)
    is_start = (pl.program_id(1) == 0).astype(jnp.int32)
    for bi in range(Q_BLOCKS):
        rows = slice(bi * BLOCK, (bi + 1) * BLOCK)
        stats = jnp.ones((BLOCK, LANES), F32)
        for p in range(N_PAIRS):
            qb = q_ref[p, rows, :]
            zero = jnp.zeros_like(qb)
            q2 = jnp.concatenate([jnp.where(low, qb, zero), jnp.where(low, zero, qb)], axis=0)
            k_cur = kt_ref[p, bi]
            v_cur = v_ref[p, rows, :]
            if bi % seq_blocks > 0:
                k_prev, v_prev, bias = kt_ref[p, bi - 1], v_ref[p, (bi - 1) * BLOCK:bi * BLOCK, :], bias_ref[0, p]
            elif mode == "chain":
                k_prev, v_prev, bias = ktp_ref[p, 0], vp_ref[p], bias_ref[is_start, p]
            else:
                k_prev, v_prev, bias = k_cur, v_cur, bias_ref[1, p]
            kk = jnp.concatenate([k_prev, k_cur], axis=1)
            vv = jnp.concatenate([jnp.concatenate([v_prev, v_cur], axis=0), ones_cols], axis=1)
            s = jnp.dot(q2, kk, preferred_element_type=F32) + bias
            m = jnp.max(s, axis=-1, keepdims=True)
            pe = jnp.exp(s - m).astype(BF16)
            r = jnp.dot(pe, vv, preferred_element_type=F32)
            o_ref[p, rows, :] = jnp.where(low, r[0:BLOCK, 0:LANES], r[BLOCK:2 * BLOCK, 0:LANES]).astype(BF16)
            stats = jnp.where(lane == 2 * p, m[0:BLOCK], stats)
            stats = jnp.where(lane == 2 * p + 1, m[BLOCK:2 * BLOCK], stats)
            stats = jnp.where(lane == N_HEADS + 2 * p, r[0:BLOCK, LANES:2 * LANES], stats)
            stats = jnp.where(lane == N_HEADS + 2 * p + 1, r[BLOCK:2 * BLOCK, LANES:2 * LANES], stats)
        st_ref[rows, :] = stats


def _attn_single_kernel(k_ref, qt_ref, vt_ref, bias_ref, o_ref, st_ref, stt):
    zeros = jnp.zeros((HEAD_DIM, BLOCK), BF16)
    stt[...] = jnp.ones(stt.shape, F32)
    for bi in range(Q_BLOCKS):
        rows = slice(bi * BLOCK, (bi + 1) * BLOCK)
        for p in range(N_PAIRS):
            qt = qt_ref[p, bi]
            q2t = jnp.concatenate(
                [jnp.concatenate([qt[0:HEAD_DIM], zeros], axis=0),
                 jnp.concatenate([zeros, qt[HEAD_DIM:2 * HEAD_DIM]], axis=0)], axis=1)
            st = jnp.dot(k_ref[p, rows, :], q2t, preferred_element_type=F32) + bias_ref[p]
            m = jnp.max(st, axis=0, keepdims=True)
            pe = jnp.exp(st - m)
            den = jnp.sum(pe, axis=0, keepdims=True)
            ot = jnp.dot(vt_ref[p, bi], pe.astype(BF16), preferred_element_type=F32)
            o_t = jnp.concatenate([ot[0:HEAD_DIM, 0:BLOCK], ot[HEAD_DIM:2 * HEAD_DIM, BLOCK:2 * BLOCK]], axis=0)
            o_ref[p, rows, :] = o_t.T.astype(BF16)
            stt[2 * p:2 * p + 1, :] = m[:, 0:BLOCK]
            stt[2 * p + 1:2 * p + 2, :] = m[:, BLOCK:2 * BLOCK]
            stt[N_HEADS + 2 * p:N_HEADS + 2 * p + 1, :] = den[:, 0:BLOCK]
            stt[N_HEADS + 2 * p + 1:N_HEADS + 2 * p + 2, :] = den[:, BLOCK:2 * BLOCK]
        st_ref[rows, :] = stt[...].T


def _attention_group(g, dilation, tok_major, feat_major, plain):
    nb, _, s, _ = tok_major.shape
    length = s // dilation
    tm = ATTN_TILE
    gen_t, first_t = _bias_tables(dilation)
    const = lambda shape: pl.BlockSpec(shape, lambda b, t: (0,) * len(shape))
    pm_spec = pl.BlockSpec((None, N_PAIRS, tm, LANES), lambda b, t: (b, 0, t, 0))
    fm_spec = pl.BlockSpec((None, N_PAIRS, Q_BLOCKS, LANES, BLOCK), lambda b, t: (b, 0, t, 0, 0))
    scratch = []
    if length == BLOCK:
        body = _attn_single_kernel
        in_specs = [pm_spec, fm_spec, fm_spec, const(first_t.shape)]
        args = [tok_major, feat_major, plain, jnp.asarray(first_t)]
        scratch = [pltpu.VMEM((BLOCK, LANES), F32)]
    else:
        gen = np.ascontiguousarray(np.concatenate(
            [gen_t[..., 0:BLOCK].swapaxes(2, 3), gen_t[..., BLOCK:].swapaxes(2, 3)], axis=2))
        in_specs = [pm_spec, fm_spec, pm_spec]
        args = [tok_major, feat_major, plain]
        if length > tm:
            mode = "chain"
            prev = lambda t: jnp.maximum(t * Q_BLOCKS - 1, 0)
            in_specs += [
                pl.BlockSpec((None, N_PAIRS, 1, LANES, BLOCK), lambda b, t: (b, 0, prev(t), 0, 0)),
                pl.BlockSpec((None, N_PAIRS, BLOCK, LANES), lambda b, t: (b, 0, prev(t), 0)),
            ]
            args += [feat_major, plain]
        else:
            assert tm % length == 0
            mode = "class"
        in_specs.append(const(gen.shape))
        args.append(jnp.asarray(gen))
        body = functools.partial(_attn_banded_kernel, mode=mode, seq_blocks=min(length, tm) // BLOCK)
    return pl.pallas_call(
        body,
        grid=(nb, s // tm),
        in_specs=in_specs,
        out_specs=[pm_spec, pl.BlockSpec((None, tm, LANES), lambda b, t: (b, t, 0))],
        out_shape=[jax.ShapeDtypeStruct((nb, N_PAIRS, s, LANES), BF16), jax.ShapeDtypeStruct((nb, s, LANES), F32)],
        scratch_shapes=scratch,
        compiler_params=_cparams(("arbitrary", "arbitrary")),
        name=f"attn_group{g}",
    )(*args)


def _merge_kernel(o1_ref, o2_ref, o3_ref, s1_ref, s2_ref, s3_ref, h_ref, wz_ref, x_ref, mod_ref, wout_ref,
                  unperm_ref, out_ref, stn2, stn3):
    o_refs = [o1_ref, o2_ref, o3_ref]
    st_refs = [s1_ref, s2_ref, s3_ref]
    stn_bufs = [None, stn2, stn3]
    stats, ogs = [], []
    for g in range(3):
        dil, per_class = st_refs[g].shape[0], st_refs[g].shape[1]
        if dil == 1:
            stats.append(st_refs[g][0])
            ogs.append(jnp.concatenate([o_refs[g][p, 0] for p in range(N_PAIRS)], axis=1).astype(F32))
            continue
        for r in range(dil):
            stn_bufs[g][pl.ds(r, per_class, stride=dil), :] = st_refs[g][r]
        stats.append(stn_bufs[g][...])
        span = PERM_ROWS // dil
        blocks = []
        for b in range(dil * per_class // PERM_ROWS):
            xb = jnp.concatenate(
                [jnp.concatenate([o_refs[g][p, r, b * span:(b + 1) * span, :] for r in range(dil)], axis=0)
                 for p in range(N_PAIRS)], axis=1)
            blocks.append(jnp.dot(unperm_ref[g - 1], xb, preferred_element_type=F32))
        ogs.append(jnp.concatenate(blocks, axis=0))
    dens = [pltpu.roll(st, LANES - N_HEADS, axis=1) for st in stats]
    m = jnp.maximum(jnp.maximum(stats[0], stats[1]), stats[2])
    es = [jnp.exp(st - m) for st in stats]
    den = es[0] * dens[0] + es[1] * dens[1] + es[2] * dens[2]
    inv = 1.0 / den
    lane = lax.broadcasted_iota(jnp.int32, m.shape, 1)
    acc = None
    for g in range(3):
        w = es[g] * inv
        wx = jnp.concatenate(
            [jnp.take_along_axis(w, jnp.where(lane < HEAD_DIM, 2 * p, 2 * p + 1), axis=1)
             for p in range(N_PAIRS)], axis=1)
        acc = wx * ogs[g] if acc is None else acc + wx * ogs[g]
    z = jnp.dot(h_ref[...], wz_ref[...].astype(BF16), preferred_element_type=F32)
    a = (acc * _silu(z)).astype(BF16)
    y = jnp.dot(a, wout_ref[...], preferred_element_type=F32)
    out_ref[...] = x_ref[...] + mod_ref[2:3, :] * y


def _merge(os_, stats, h1, w_all, z_col, x1, mod, w_out):
    nb, s, d = x1.shape
    tm = TOK_TILE
    x_spec = pl.BlockSpec((None, tm, d), lambda b, t: (b, t, 0))
    const = lambda shape: pl.BlockSpec(shape, lambda b, t: (0,) * len(shape))
    o_views, o_specs, st_views, st_specs, unperm = [], [], [], [], []
    for (_, dil), o, st in zip(DILATED_GROUPS, os_, stats):
        o_views.append(o.reshape(nb, N_PAIRS, dil, s // dil, LANES))
        o_specs.append(pl.BlockSpec((None, N_PAIRS, dil, tm // dil, LANES), lambda b, t: (b, 0, 0, t, 0)))
        st_views.append(st.reshape(nb, dil, s // dil, LANES))
        st_specs.append(pl.BlockSpec((None, dil, tm // dil, LANES), lambda b, t: (b, 0, t, 0)))
        if dil > 1:
            span = PERM_ROWS // dil
            pt = np.zeros((PERM_ROWS, PERM_ROWS), np.float32)
            for r in range(dil):
                for i in range(span):
                    pt[i * dil + r, r * span + i] = 1.0
            unperm.append(pt)
    unperm = jnp.asarray(np.stack(unperm), BF16)
    return pl.pallas_call(
        _merge_kernel,
        grid=(nb, s // tm),
        in_specs=o_specs + st_specs + [
            pl.BlockSpec((None, tm, d), lambda b, t: (b, t, 0)),
            pl.BlockSpec((d, d), lambda b, t: (0, z_col), pipeline_mode=pl.Buffered(1)),
            x_spec, pl.BlockSpec((None, None, 3, d), lambda b, t: (1, b, 0, 0)),
            const((d, d)), const(unperm.shape)],
        out_specs=x_spec,
        out_shape=jax.ShapeDtypeStruct((nb, s, d), F32),
        scratch_shapes=[pltpu.VMEM((tm, LANES), F32), pltpu.VMEM((tm, LANES), F32)],
        compiler_params=_cparams(("arbitrary", "arbitrary")),
        name="merge_out_proj",
    )(*o_views, *st_views, h1, w_all, x1, mod, w_out.astype(BF16), unperm)


def kernel(x, c, norm_g, ada_w, ada_b, a_w_in, a_conv_w, a_conv_b, a_ln_g, a_ln_b, a_w_out, b_w_in, b_q_norm,
           b_k_norm, b_w_out):
    d = D_MODEL
    s = x.shape[1]
    mod = _ada_modulation(c, ada_w, ada_b)
    x1, *h1s = _layer0(x, mod, norm_g, a_w_in[0], a_conv_w[0], a_conv_b[0], a_ln_g[0], a_ln_b[0], a_w_out[0])
    h1s = [h.reshape(x.shape) for h in h1s]

    w_b = b_w_in[0]
    z_col = 3 * len(DILATED_GROUPS)
    scale = HEAD_DIM ** -0.5
    os_, stats = [], []
    for g, (_, dilation) in enumerate(DILATED_GROUPS):
        q_col, k_col, v_col = 3 * g, 3 * g + 1, 3 * g + 2
        gq, gk = b_q_norm[0, g] * scale, b_k_norm[0, g]
        row_gain = lambda gain: jnp.tile(gain, 2).reshape(1, LANES)
        col_gain = lambda gain: jnp.broadcast_to(gain[None, :, None], (N_HEADS, HEAD_DIM, BLOCK))
        if s // dilation == BLOCK:
            outs = _project_group(h1s[g], g, w_b, (k_col, q_col, v_col), row_gain(gk), col_gain(gq), True)
        else:
            outs = _project_group(h1s[g], g, w_b, (q_col, k_col, v_col), row_gain(gq), col_gain(gk), False)
        o, st = _attention_group(g, dilation, *outs)
        os_.append(o)
        stats.append(st)
    return _merge(os_, stats, h1s[0], w_b, z_col, x1, mod, b_w_out[0])
```

```python
import functools

import jax
import jax.numpy as jnp
import numpy as np
from jax import lax
from jax.experimental import pallas as pl
from jax.experimental.pallas import tpu as pltpu

F32 = jnp.float32
BF16 = jnp.bfloat16

D_MODEL = 1024
CONV_WIDTH = 31
HEAD_DIM = 64
N_HEADS = 16
N_PAIRS = N_HEADS // 2
DILATED_GROUPS = ((128, 1), (512, 4), (2048, 16))
BLOCK = 128
NORM_EPS = 1e-6
NEG_INF = -1e30
LANES = 128
SUBLANES = 8
VMEM_LIMIT = 56 * 1024 * 1024

TOK_TILE = 512
PERM_ROWS = 256
PROJ_TILE = 1024
ATTN_TILE = 1024
Q_BLOCKS = ATTN_TILE // BLOCK
L0_TILE = 512
HALO = 32
CONV_ROWS = 128


def _sigmoid(x):
    return jax.nn.sigmoid(x)


def _silu(x):
    return x * _sigmoid(x)


def _cparams(sem):
    return pltpu.CompilerParams(dimension_semantics=sem, vmem_limit_bytes=VMEM_LIMIT)


def _ada_kernel(c_ref, w_ref, b_ref, o_ref):
    c = c_ref[...]
    sc = _silu(c).astype(BF16)
    o_ref[...] = jnp.dot(sc, w_ref[...].astype(BF16), preferred_element_type=F32) + b_ref[...]


def _ada_modulation(c, ada_w, ada_b):
    depth, d, d3 = ada_w.shape
    nb = c.shape[0]
    cols = d3 // 2
    out = pl.pallas_call(
        _ada_kernel,
        grid=(depth, d3 // cols),
        in_specs=[
            pl.BlockSpec((nb, d), lambda l, j: (0, 0)),
            pl.BlockSpec((None, d, cols), lambda l, j: (l, 0, j)),
            pl.BlockSpec((None, 1, cols), lambda l, j: (l, 0, j)),
        ],
        out_specs=pl.BlockSpec((None, nb, cols), lambda l, j: (l, 0, j)),
        out_shape=jax.ShapeDtypeStruct((depth, nb, d3), F32),
        compiler_params=_cparams(("arbitrary", "arbitrary")),
        name="ada_modulation",
    )(c, ada_w, ada_b.reshape(depth, 1, d3))
    return out.reshape(depth, nb, 3, d)


def _layer0_kernel(x_ref, mod0_ref, mod1_ref, g0_ref, g1_ref, win_ref, cw_ref, cb_ref, lg_ref, lb_ref,
                   wout_ref, perm_ref, x1_ref, h1_ref, h1c4_ref, h1c16_ref, ubuf, sbuf, cbuf, szbuf):
    ts = x_ref.shape[0]
    d = x_ref.shape[1]

    @pl.when(pl.program_id(1) == 0)
    def _():
        ubuf[0:HALO, :] = jnp.zeros((HALO, d), F32)

    shift, scale, gate = mod0_ref[0:1, :], mod0_ref[1:2, :], mod0_ref[2:3, :]
    gain0 = g0_ref[...] * (1.0 + scale)
    off = HALO - (CONV_WIDTH - 1)
    shifted_rows = sbuf.shape[1]
    x = x_ref[...]
    ms = jnp.mean(x * x, axis=-1, keepdims=True)
    h = x * lax.rsqrt(ms + NORM_EPS) * gain0 + shift
    proj = jnp.dot(h.astype(BF16), win_ref[...], preferred_element_type=F32)
    ubuf[HALO:HALO + ts, :] = proj[:, 0:d] * _sigmoid(proj[:, d:2 * d])
    szbuf[...] = _silu(proj[:, 2 * d:3 * d])

    for s in range(1, SUBLANES):
        sbuf[s - 1] = ubuf[s:s + shifted_rows, :]
    for ls in range(d // LANES):
        lsl = slice(ls * LANES, (ls + 1) * LANES)
        for rc in range(ts // CONV_ROWS):
            r0 = rc * CONV_ROWS
            acc = jnp.broadcast_to(cb_ref[0:1, lsl], (CONV_ROWS, LANES))
            for j in range(CONV_WIDTH):
                s = (off + j) % SUBLANES
                a = r0 + off + j - s
                if s == 0:
                    tap = ubuf[a:a + CONV_ROWS, lsl]
                else:
                    tap = sbuf[s - 1, a:a + CONV_ROWS, lsl]
                acc = acc + cw_ref[j:j + 1, lsl] * tap
            cbuf[r0:r0 + CONV_ROWS, lsl] = acc
    ubuf[0:HALO, :] = ubuf[ts:ts + HALO, :]

    shift1, scale1 = mod1_ref[0:1, :], mod1_ref[1:2, :]
    gain1 = g1_ref[...] * (1.0 + scale1)
    for b in range(ts // PERM_ROWS):
        rows = slice(b * PERM_ROWS, (b + 1) * PERM_ROWS)
        cv = cbuf[rows, :]
        mu = jnp.mean(cv, axis=-1, keepdims=True)
        xc = cv - mu
        var = jnp.mean(xc * xc, axis=-1, keepdims=True)
        y = xc * lax.rsqrt(var + NORM_EPS) * lg_ref[...] + lb_ref[...]
        a = (_silu(y) * szbuf[rows, :]).astype(BF16)
        out = jnp.dot(a, wout_ref[...], preferred_element_type=F32)
        x1 = x_ref[rows, :] + gate * out
        x1_ref[rows, :] = x1

        ms1 = jnp.mean(x1 * x1, axis=-1, keepdims=True)
        h1b = (x1 * lax.rsqrt(ms1 + NORM_EPS) * gain1 + shift1).astype(BF16)
        h1_ref[rows, :] = h1b
        hp = jnp.dot(perm_ref[...], h1b, preferred_element_type=F32).astype(BF16)
        for k, ref in enumerate((h1c4_ref, h1c16_ref)):
            span = PERM_ROWS // ref.shape[0]
            ref[:, b * span:(b + 1) * span, :] = hp[k * PERM_ROWS:(k + 1) * PERM_ROWS].reshape(
                ref.shape[0], span, d)


def _layer0(x, mod, norm_g, w_in, conv_w, conv_b, ln_g, ln_b, w_out):
    nb, s, d = x.shape
    ts = L0_TILE
    dils = [dil for _, dil in DILATED_GROUPS if dil > 1]
    row = lambda a: a.reshape(1, d)
    const = lambda shape: pl.BlockSpec(shape, lambda b, t: (0,) * len(shape))
    once = lambda shape: pl.BlockSpec(shape, lambda b, t: (0,) * len(shape), pipeline_mode=pl.Buffered(1))
    perm = np.zeros((len(dils) * PERM_ROWS, PERM_ROWS), np.float32)
    for k, dil in enumerate(dils):
        for r in range(dil):
            for i in range(PERM_ROWS // dil):
                perm[k * PERM_ROWS + r * (PERM_ROWS // dil) + i, i * dil + r] = 1.0
    return pl.pallas_call(
        _layer0_kernel,
        grid=(nb, s // ts),
        in_specs=[
            pl.BlockSpec((None, ts, d), lambda b, t: (b, t, 0)),
            pl.BlockSpec((None, None, 3, d), lambda b, t: (0, b, 0, 0)),
            pl.BlockSpec((None, None, 3, d), lambda b, t: (1, b, 0, 0)),
            const((1, d)), const((1, d)),
            once((d, 3 * d)),
            const((CONV_WIDTH, d)), const((1, d)), const((1, d)), const((1, d)),
            once((d, d)), once(perm.shape),
        ],
        out_specs=[
            pl.BlockSpec((None, ts, d), lambda b, t: (b, t, 0)),
            pl.BlockSpec((None, ts, d), lambda b, t: (b, t, 0)),
        ] + [pl.BlockSpec((None, dil, ts // dil, d), lambda b, t: (b, 0, t, 0)) for dil in dils],
        out_shape=[jax.ShapeDtypeStruct((nb, s, d), F32), jax.ShapeDtypeStruct((nb, s, d), BF16)]
        + [jax.ShapeDtypeStruct((nb, dil, s // dil, d), BF16) for dil in dils],
        scratch_shapes=[pltpu.VMEM((HALO + ts, d), F32),
                        pltpu.VMEM((SUBLANES - 1, HALO + ts - SUBLANES, d), F32),
                        pltpu.VMEM((ts, d), F32), pltpu.VMEM((ts, d), F32)],
        compiler_params=_cparams(("arbitrary", "arbitrary")),
        name="layer0_conv_mixer",
    )(x, mod, mod, row(norm_g[0]), row(norm_g[1]), w_in.astype(BF16), conv_w, row(conv_b), row(ln_g),
      row(ln_b), w_out.astype(BF16), jnp.asarray(perm, BF16))


def _store_rows(ref, x):
    for p in range(N_PAIRS):
        ref[p] = x[:, p * LANES:(p + 1) * LANES]


def _store_cols(ref, xt):
    for tb in range(xt.shape[1] // BLOCK):
        for p in range(N_PAIRS):
            ref[p, tb] = xt[p * LANES:(p + 1) * LANES, tb * BLOCK:(tb + 1) * BLOCK]


def _proj_kernel(h_ref, wn_ref, wt_ref, wp_ref, gn_ref, gt_ref, n_ref, t_ref, p_ref, *, plain_transposed):
    h = h_ref[...]
    tm = h.shape[0]
    tn_dims = (((0,), (1,)), ((), ()))
    lane = lax.broadcasted_iota(jnp.int32, (tm, LANES), 1)
    low = lane < HEAD_DIM

    xn = jnp.dot(h, wn_ref[...].astype(BF16), preferred_element_type=F32)
    for p in range(N_PAIRS):
        xq = xn[:, p * LANES:(p + 1) * LANES]
        x2 = xq * xq
        s_lo = jnp.sum(jnp.where(low, x2, 0.0), axis=-1, keepdims=True)
        s_hi = jnp.sum(jnp.where(low, 0.0, x2), axis=-1, keepdims=True)
        inv = lax.rsqrt(jnp.where(low, s_lo, s_hi) * (1.0 / HEAD_DIM) + NORM_EPS)
        n_ref[p] = (xq * inv * gn_ref[...]).astype(BF16)

    xt = lax.dot_general(wt_ref[...].astype(BF16), h, tn_dims, preferred_element_type=F32)
    for tb in range(tm // BLOCK):
        blk = xt[:, tb * BLOCK:(tb + 1) * BLOCK].reshape(N_HEADS, HEAD_DIM, BLOCK)
        ss = jnp.sum(blk * blk, axis=1, keepdims=True)
        tn = blk * lax.rsqrt(ss * (1.0 / HEAD_DIM) + NORM_EPS) * gt_ref[...]
        tn = tn.reshape(N_PAIRS, 2 * HEAD_DIM, BLOCK).astype(BF16)
        for p in range(N_PAIRS):
            t_ref[p, tb] = tn[p]

    wp = wp_ref[...].astype(BF16)
    if plain_transposed:
        _store_cols(p_ref, lax.dot_general(wp, h, tn_dims, preferred_element_type=F32).astype(BF16))
    else:
        _store_rows(p_ref, jnp.dot(h, wp, preferred_element_type=F32).astype(BF16))


def _project_group(h, g, w_all, cols, gn, gt, plain_transposed):
    nb, s, d = h.shape
    tm = PROJ_TILE
    steps = s // tm
    const = lambda shape: pl.BlockSpec(shape, lambda b, t: (0,) * len(shape))
    w_spec = lambda col: pl.BlockSpec((d, d), lambda b, t: (0, col), pipeline_mode=pl.Buffered(1))
    in_specs = [pl.BlockSpec((None, tm, d), lambda b, t: (b, t, 0)), w_spec(cols[0]), w_spec(cols[1]),
                w_spec(cols[2]), const((1, LANES)), const((N_HEADS, HEAD_DIM, BLOCK))]
    args = [h, w_all, w_all, w_all, gn, gt]
    pm_spec = pl.BlockSpec((None, N_PAIRS, tm, LANES), lambda b, t: (b, 0, t, 0))
    pm_shape = jax.ShapeDtypeStruct((nb, N_PAIRS, s, LANES), BF16)
    fm_spec = pl.BlockSpec((None, N_PAIRS, tm // BLOCK, LANES, BLOCK), lambda b, t: (b, 0, t, 0, 0))
    fm_shape = jax.ShapeDtypeStruct((nb, N_PAIRS, s // BLOCK, LANES, BLOCK), BF16)
    out_specs = [pm_spec, fm_spec, fm_spec if plain_transposed else pm_spec]
    out_shape = [pm_shape, fm_shape, fm_shape if plain_transposed else pm_shape]
    return pl.pallas_call(
        functools.partial(_proj_kernel, plain_transposed=plain_transposed),
        grid=(nb, steps),
        in_specs=in_specs,
        out_specs=out_specs,
        out_shape=out_shape,
        compiler_params=_cparams(("arbitrary", "arbitrary")),
        name=f"proj_group{g}",
    )(*args)


def _alibi_slopes():
    return np.exp2(-8.0 * np.arange(1, N_HEADS + 1, dtype=np.float64) / N_HEADS)


def _bias_tables(dilation):
    kj = np.arange(2 * BLOCK)[:, None]
    qi = np.arange(BLOCK)[None, :]
    steps = qi + BLOCK - kj
    n_steps = BLOCK
    valid = (steps >= 0) & (steps <= n_steps)
    valid_first = valid & (kj >= BLOCK)
    dist = (steps * dilation).astype(np.float64)
    slopes = _alibi_slopes()
    gen = np.empty((2, N_PAIRS, 2 * BLOCK, 2 * BLOCK), np.float32)
    for p in range(N_PAIRS):
        for half in range(2):
            sl = slopes[2 * p + half]
            cols = slice(half * BLOCK, (half + 1) * BLOCK)
            gen[0, p, :, cols] = np.where(valid, -sl * dist, NEG_INF)
            gen[1, p, :, cols] = np.where(valid_first, -sl * dist, NEG_INF)
    first = np.ascontiguousarray(gen[1, :, BLOCK:, :])
    return gen, first


def _attn_banded_kernel(*refs, mode, seq_blocks):
    if mode == "chain":
        q_ref, kt_ref, v_ref, ktp_ref, vp_ref, bias_ref, o_ref, st_ref = refs
    else:
        q_ref, kt_ref, v_ref, bias_ref, o_ref, st_ref = refs
    lane = lax.broadcasted_iota(jnp.int32, (BLOCK, LANES), 1)
    low = lane < HEAD_DIM
    ones_cols = jnp.ones((2 * BLOCK, LANES), BF16)
    is_start = (pl.program_id(1) == 0).astype(jnp.int32)
    for bi in range(Q_BLOCKS):
        rows = slice(bi * BLOCK, (bi + 1) * BLOCK)
        stats = jnp.ones((BLOCK, LANES), F32)
        for p in range(N_PAIRS):
            qb = q_ref[p, rows, :]
            zero = jnp.zeros_like(qb)
            q2 = jnp.concatenate([jnp.where(low, qb, zero), jnp.where(low, zero, qb)], axis=0)
            k_cur = kt_ref[p, bi]
            v_cur = v_ref[p, rows, :]
            if bi % seq_blocks > 0:
                k_prev, v_prev, bias = kt_ref[p, bi - 1], v_ref[p, (bi - 1) * BLOCK:bi * BLOCK, :], bias_ref[0, p]
            elif mode == "chain":
                k_prev, v_prev, bias = ktp_ref[p, 0], vp_ref[p], bias_ref[is_start, p]
            else:
                k_prev, v_prev, bias = k_cur, v_cur, bias_ref[1, p]
            kk = jnp.concatenate([k_prev, k_cur], axis=1)
            vv = jnp.concatenate([jnp.concatenate([v_prev, v_cur], axis=0), ones_cols], axis=1)
            s = jnp.dot(q2, kk, preferred_element_type=F32) + bias
            m = jnp.max(s, axis=-1, keepdims=True)
            pe = jnp.exp(s - m).astype(BF16)
            r = jnp.dot(pe, vv, preferred_element_type=F32)
            o_ref[p, rows, :] = jnp.where(low, r[0:BLOCK, 0:LANES], r[BLOCK:2 * BLOCK, 0:LANES]).astype(BF16)
            stats = jnp.where(lane == 2 * p, m[0:BLOCK], stats)
            stats = jnp.where(lane == 2 * p + 1, m[BLOCK:2 * BLOCK], stats)
            stats = jnp.where(lane == N_HEADS + 2 * p, r[0:BLOCK, LANES:2 * LANES], stats)
            stats = jnp.where(lane == N_HEADS + 2 * p + 1, r[BLOCK:2 * BLOCK, LANES:2 * LANES], stats)
        st_ref[rows, :] = stats


def _attn_single_kernel(k_ref, qt_ref, vt_ref, bias_ref, o_ref, st_ref, stt):
    zeros = jnp.zeros((HEAD_DIM, BLOCK), BF16)
    stt[...] = jnp.ones(stt.shape, F32)
    for bi in range(Q_BLOCKS):
        rows = slice(bi * BLOCK, (bi + 1) * BLOCK)
        for p in range(N_PAIRS):
            qt = qt_ref[p, bi]
            q2t = jnp.concatenate(
                [jnp.concatenate([qt[0:HEAD_DIM], zeros], axis=0),
                 jnp.concatenate([zeros, qt[HEAD_DIM:2 * HEAD_DIM]], axis=0)], axis=1)
            st = jnp.dot(k_ref[p, rows, :], q2t, preferred_element_type=F32) + bias_ref[p]
            m = jnp.max(st, axis=0, keepdims=True)
            pe = jnp.exp(st - m)
            den = jnp.sum(pe, axis=0, keepdims=True)
            ot = jnp.dot(vt_ref[p, bi], pe.astype(BF16), preferred_element_type=F32)
            o_t = jnp.concatenate([ot[0:HEAD_DIM, 0:BLOCK], ot[HEAD_DIM:2 * HEAD_DIM, BLOCK:2 * BLOCK]], axis=0)
            o_ref[p, rows, :] = o_t.T.astype(BF16)
            stt[2 * p:2 * p + 1, :] = m[:, 0:BLOCK]
            stt[2 * p + 1:2 * p + 2, :] = m[:, BLOCK:2 * BLOCK]
            stt[N_HEADS + 2 * p:N_HEADS + 2 * p + 1, :] = den[:, 0:BLOCK]
            stt[N_HEADS + 2 * p + 1:N_HEADS + 2 * p + 2, :] = den[:, BLOCK:2 * BLOCK]
        st_ref[rows, :] = stt[...].T


def _attention_group(g, dilation, tok_major, feat_major, plain):
    nb, _, s, _ = tok_major.shape
    length = s // dilation
    tm = ATTN_TILE
    gen_t, first_t = _bias_tables(dilation)
    const = lambda shape: pl.BlockSpec(shape, lambda b, t: (0,) * len(shape))
    pm_spec = pl.BlockSpec((None, N_PAIRS, tm, LANES), lambda b, t: (b, 0, t, 0))
    fm_spec = pl.BlockSpec((None, N_PAIRS, Q_BLOCKS, LANES, BLOCK), lambda b, t: (b, 0, t, 0, 0))
    scratch = []
    if length == BLOCK:
        body = _attn_single_kernel
        in_specs = [pm_spec, fm_spec, fm_spec, const(first_t.shape)]
        args = [tok_major, feat_major, plain, jnp.asarray(first_t)]
        scratch = [pltpu.VMEM((BLOCK, LANES), F32)]
    else:
        gen = np.ascontiguousarray(np.concatenate(
            [gen_t[..., 0:BLOCK].swapaxes(2, 3), gen_t[..., BLOCK:].swapaxes(2, 3)], axis=2))
        in_specs = [pm_spec, fm_spec, pm_spec]
        args = [tok_major, feat_major, plain]
        if length > tm:
            mode = "chain"
            prev = lambda t: jnp.maximum(t * Q_BLOCKS - 1, 0)
            in_specs += [
                pl.BlockSpec((None, N_PAIRS, 1, LANES, BLOCK), lambda b, t: (b, 0, prev(t), 0, 0)),
                pl.BlockSpec((None, N_PAIRS, BLOCK, LANES), lambda b, t: (b, 0, prev(t), 0)),
            ]
            args += [feat_major, plain]
        else:
            assert tm % length == 0
            mode = "class"
        in_specs.append(const(gen.shape))
        args.append(jnp.asarray(gen))
        body = functools.partial(_attn_banded_kernel, mode=mode, seq_blocks=min(length, tm) // BLOCK)
    return pl.pallas_call(
        body,
        grid=(nb, s // tm),
        in_specs=in_specs,
        out_specs=[pm_spec, pl.BlockSpec((None, tm, LANES), lambda b, t: (b, t, 0))],
        out_shape=[jax.ShapeDtypeStruct((nb, N_PAIRS, s, LANES), BF16), jax.ShapeDtypeStruct((nb, s, LANES), F32)],
        scratch_shapes=scratch,
        compiler_params=_cparams(("arbitrary", "arbitrary")),
        name=f"attn_group{g}",
    )(*args)


def _merge_kernel(o1_ref, o2_ref, o3_ref, s1_ref, s2_ref, s3_ref, h_ref, wz_ref, x_ref, mod_ref, wout_ref,
                  unperm_ref, out_ref, stn2, stn3):
    o_refs = [o1_ref, o2_ref, o3_ref]
    st_refs = [s1_ref, s2_ref, s3_ref]
    stn_bufs = [None, stn2, stn3]
    tm = x_ref.shape[0]
    stats = []
    for g in range(3):
        dil, per_class = st_refs[g].shape[0], st_refs[g].shape[1]
        if dil == 1:
            stats.append(st_refs[g][0])
            continue
        for r in range(dil):
            stn_bufs[g][pl.ds(r, per_class, stride=dil), :] = st_refs[g][r]
        stats.append(stn_bufs[g][...])
    dens = [pltpu.roll(st, LANES - N_HEADS, axis=1) for st in stats]
    m = jnp.maximum(jnp.maximum(stats[0], stats[1]), stats[2])
    es = [jnp.exp(st - m) for st in stats]
    den = es[0] * dens[0] + es[1] * dens[1] + es[2] * dens[2]
    inv = 1.0 / den
    ws = [e * inv for e in es]
    lane = lax.broadcasted_iota(jnp.int32, (PERM_ROWS, LANES), 1)
    wz = wz_ref[...].astype(BF16)
    for b in range(tm // PERM_ROWS):
        rows = slice(b * PERM_ROWS, (b + 1) * PERM_ROWS)
        acc = None
        for g in range(3):
            dil = st_refs[g].shape[0]
            if dil == 1:
                og = jnp.concatenate([o_refs[g][p, 0, rows, :] for p in range(N_PAIRS)], axis=1).astype(F32)
            else:
                span = PERM_ROWS // dil
                xb = jnp.concatenate(
                    [jnp.concatenate([o_refs[g][p, r, b * span:(b + 1) * span, :] for r in range(dil)], axis=0)
                     for p in range(N_PAIRS)], axis=1)
                og = jnp.dot(unperm_ref[g - 1], xb, preferred_element_type=F32)
            w = ws[g][rows]
            wx = jnp.concatenate(
                [jnp.take_along_axis(w, jnp.where(lane < HEAD_DIM, 2 * p, 2 * p + 1), axis=1)
                 for p in range(N_PAIRS)], axis=1)
            acc = wx * og if acc is None else acc + wx * og
        z = jnp.dot(h_ref[rows, :], wz, preferred_element_type=F32)
        a = (acc * _silu(z)).astype(BF16)
        y = jnp.dot(a, wout_ref[...], preferred_element_type=F32)
        out_ref[rows, :] = x_ref[rows, :] + mod_ref[2:3, :] * y


def _merge(os_, stats, h1, w_all, z_col, x1, mod, w_out):
    nb, s, d = x1.shape
    tm = TOK_TILE
    x_spec = pl.BlockSpec((None, tm, d), lambda b, t: (b, t, 0))
    const = lambda shape: pl.BlockSpec(shape, lambda b, t: (0,) * len(shape))
    o_views, o_specs, st_views, st_specs, unperm = [], [], [], [], []
    for (_, dil), o, st in zip(DILATED_GROUPS, os_, stats):
        o_views.append(o.reshape(nb, N_PAIRS, dil, s // dil, LANES))
        o_specs.append(pl.BlockSpec((None, N_PAIRS, dil, tm // dil, LANES), lambda b, t: (b, 0, 0, t, 0)))
        st_views.append(st.reshape(nb, dil, s // dil, LANES))
        st_specs.append(pl.BlockSpec((None, dil, tm // dil, LANES), lambda b, t: (b, 0, t, 0)))
        if dil > 1:
            span = PERM_ROWS // dil
            pt = np.zeros((PERM_ROWS, PERM_ROWS), np.float32)
            for r in range(dil):
                for i in range(span):
                    pt[i * dil + r, r * span + i] = 1.0
            unperm.append(pt)
    unperm = jnp.asarray(np.stack(unperm), BF16)
    return pl.pallas_call(
        _merge_kernel,
        grid=(nb, s // tm),
        in_specs=o_specs + st_specs + [
            pl.BlockSpec((None, tm, d), lambda b, t: (b, t, 0)),
            pl.BlockSpec((d, d), lambda b, t: (0, z_col), pipeline_mode=pl.Buffered(1)),
            x_spec, pl.BlockSpec((None, None, 3, d), lambda b, t: (1, b, 0, 0)),
            const((d, d)), const(unperm.shape)],
        out_specs=x_spec,
        out_shape=jax.ShapeDtypeStruct((nb, s, d), F32),
        scratch_shapes=[pltpu.VMEM((tm, LANES), F32), pltpu.VMEM((tm, LANES), F32)],
        compiler_params=_cparams(("arbitrary", "arbitrary")),
        name="merge_out_proj",
    )(*o_views, *st_views, h1, w_all, x1, mod, w_out.astype(BF16), unperm)


def kernel(x, c, norm_g, ada_w, ada_b, a_w_in, a_conv_w, a_conv_b, a_ln_g, a_ln_b, a_w_out, b_w_in, b_q_norm,
           b_k_norm, b_w_out):
    d = D_MODEL
    s = x.shape[1]
    mod = _ada_modulation(c, ada_w, ada_b)
    x1, *h1s = _layer0(x, mod, norm_g, a_w_in[0], a_conv_w[0], a_conv_b[0], a_ln_g[0], a_ln_b[0], a_w_out[0])
    h1s = [h.reshape(x.shape) for h in h1s]

    w_b = b_w_in[0]
    z_col = 3 * len(DILATED_GROUPS)
    scale = HEAD_DIM ** -0.5
    os_, stats = [], []
    for g, (_, dilation) in enumerate(DILATED_GROUPS):
        q_col, k_col, v_col = 3 * g, 3 * g + 1, 3 * g + 2
        gq, gk = b_q_norm[0, g] * scale, b_k_norm[0, g]
        row_gain = lambda gain: jnp.tile(gain, 2).reshape(1, LANES)
        col_gain = lambda gain: jnp.broadcast_to(gain[None, :, None], (N_HEADS, HEAD_DIM, BLOCK))
        if s // dilation == BLOCK:
            outs = _project_group(h1s[g], g, w_b, (k_col, q_col, v_col), row_gain(gk), col_gain(gq), True)
        else:
            outs = _project_group(h1s[g], g, w_b, (q_col, k_col, v_col), row_gain(gq), col_gain(gk), False)
        o, st = _attention_group(g, dilation, *outs)
        os_.append(o)
        stats.append(st)
    return _merge(os_, stats, h1s[0], w_b, z_col, x1, mod, b_w_out[0])
```
